```python
import math
import jax, jax.numpy as jnp
from jax import lax
import numpy as np

D_MODEL = 1024
BATCH = 8
SEQ = 8192
DEPTH = 1
DEC_BATCH = 1
DEC_SEQ = 16384
PAST_LEN = 128

SSM_WIDTH = D_MODEL // 2
SSM_GROUP = 16
SSM_GROUPS = SSM_WIDTH // SSM_GROUP
SSM_STATE = 64
N_HEADS = 8
N_KV_HEADS = 2
HEAD_DIM = 64
ATTN_WIDTH = N_HEADS * HEAD_DIM
KV_WIDTH = N_KV_HEADS * HEAD_DIM
MIX_WIDTH = SSM_WIDTH + ATTN_WIDTH
IN_WIDTH = SSM_WIDTH + ATTN_WIDTH + 2 * KV_WIDTH
D_FF = 2816
CONV_W = 3
GRID_W = 64
ROPE_THETA = 10000.0
Q_BLOCK = 128
EPS = 1e-6
DT_MIN = 1e-3
DT_MAX = 1e-1

kernel_name = "hymba_s5_gqa_axial_convffn_encoder"

F32 = jnp.float32


def rms_norm(x, g):
    xf = x.astype(F32)
    y = xf * lax.rsqrt(jnp.mean(xf * xf, axis=-1, keepdims=True) + EPS)
    return (y * g.astype(F32)).astype(x.dtype)


def axial_rope_tables(seq_len):
    rows = seq_len // GRID_W
    row_id = jnp.broadcast_to(jnp.arange(rows)[:, None], (rows, GRID_W)).reshape(-1).astype(F32)
    col_id = jnp.broadcast_to(jnp.arange(GRID_W)[None, :], (rows, GRID_W)).reshape(-1).astype(F32)
    axis_dim = HEAD_DIM // 2
    inv_freq = ROPE_THETA ** (-jnp.arange(0, axis_dim, 2, dtype=F32) / axis_dim)
    ang = jnp.concatenate([row_id[:, None] * inv_freq, col_id[:, None] * inv_freq], axis=-1)
    return jnp.cos(ang), jnp.sin(ang)


def apply_rope(x, cos, sin):
    xf = x.astype(F32)
    half = HEAD_DIM // 2
    x1, x2 = xf[..., :half], xf[..., half:]
    c = cos[:, None, :]
    s = sin[:, None, :]
    return jnp.concatenate([x1 * c - x2 * s, x2 * c + x1 * s], axis=-1).astype(x.dtype)


def gqa_block_attention(q, k, v):
    b, l = q.shape[0], q.shape[1]
    rep = N_HEADS // N_KV_HEADS
    nb = l // Q_BLOCK
    qb = q.reshape(b, nb, Q_BLOCK, N_KV_HEADS, rep, HEAD_DIM).transpose(1, 0, 2, 3, 4, 5)
    scale = HEAD_DIM ** -0.5

    def one_block(qblk):
        s = jnp.einsum('bqkgd,bskd->bkgqs', qblk, k).astype(F32) * scale
        p = jax.nn.softmax(s, axis=-1).astype(v.dtype)
        return jnp.einsum('bkgqs,bskd->bqkgd', p, v)

    o = lax.map(one_block, qb)
    return o.transpose(1, 0, 2, 3, 4, 5).reshape(b, l, ATTN_WIDTH)


def s5_bidirectional(u, lam_re, lam_im, log_dt, b_re, b_im, c_re, c_im, d_skip):
    bsz, l = u.shape[0], u.shape[1]
    uf = u.astype(F32).reshape(bsz, l, SSM_GROUPS, SSM_GROUP)
    dt = jnp.exp(log_dt.astype(F32))[..., None]
    lr = lam_re.astype(F32)
    li = lam_im.astype(F32)
    mag = jnp.exp(lr * dt)
    ab_re = mag * jnp.cos(li * dt)
    ab_im = mag * jnp.sin(li * dt)
    den = lr * lr + li * li
    nr = ab_re - 1.0
    ni = ab_im
    zr = (nr * lr + ni * li) / den
    zi = (ni * lr - nr * li) / den
    br = b_re.astype(F32)
    bi = b_im.astype(F32)
    bb_re = zr[..., None] * br - zi[..., None] * bi
    bb_im = zr[..., None] * bi + zi[..., None] * br
    cr = c_re.astype(F32)
    ci = c_im.astype(F32)

    def combine(e1, e2):
        a1r, a1i, x1r, x1i = e1
        a2r, a2i, x2r, x2i = e2
        return (a2r * a1r - a2i * a1i,
                a2r * a1i + a2i * a1r,
                a2r * x1r - a2i * x1i + x2r,
                a2r * x1i + a2i * x1r + x2i)

    def one_direction(useq, d, reverse):
        xr = jnp.einsum('lgh,gph->lgp', useq, bb_re[d])
        xi = jnp.einsum('lgh,gph->lgp', useq, bb_im[d])
        ar = jnp.broadcast_to(ab_re[d], xr.shape)
        ai = jnp.broadcast_to(ab_im[d], xr.shape)
        _, _, sr, si = lax.associative_scan(combine, (ar, ai, xr, xi), reverse=reverse, axis=0)
        return jnp.einsum('lgp,ghp->lgh', sr, cr[d]) - jnp.einsum('lgp,ghp->lgh', si, ci[d])

    def one_seq(useq):
        return one_direction(useq, 0, False) + one_direction(useq, 1, True)

    y = lax.map(one_seq, uf) + uf * d_skip.astype(F32)
    return y.reshape(bsz, l, SSM_WIDTH).astype(u.dtype)


def dwconv_centered(h, w, b):
    pad = CONV_W // 2
    l = h.shape[1]
    hp = jnp.pad(h, ((0, 0), (pad, pad), (0, 0)))
    out = b
    for j in range(CONV_W):
        out = out + hp[:, j:j + l] * w[j]
    return out


def encoder(x, norm1_g, w_in, lam_re, lam_im, log_dt, b_re, b_im, c_re, c_im, d_skip,
            w_glu, b_glu, q_norm_g, k_norm_g, ssm_out_g, attn_out_g, w_out,
            norm2_g, w_gate, w_up, conv_w, conv_b, w_down, final_norm_g):
    bsz, l = x.shape[0], x.shape[1]
    cos, sin = axial_rope_tables(l)
    for i in range(DEPTH):
        h = rms_norm(x, norm1_g[i])
        z = h @ w_in[i]
        u = z[..., :SSM_WIDTH]
        q = z[..., SSM_WIDTH:SSM_WIDTH + ATTN_WIDTH]
        k = z[..., SSM_WIDTH + ATTN_WIDTH:SSM_WIDTH + ATTN_WIDTH + KV_WIDTH]
        v = z[..., SSM_WIDTH + ATTN_WIDTH + KV_WIDTH:]

        s = s5_bidirectional(u, lam_re[i], lam_im[i], log_dt[i], b_re[i], b_im[i],
                             c_re[i], c_im[i], d_skip[i])
        s = jax.nn.gelu(s)
        s = s * jax.nn.sigmoid(s @ w_glu[i] + b_glu[i])

        q = q.reshape(bsz, l, N_HEADS, HEAD_DIM)
        k = k.reshape(bsz, l, N_KV_HEADS, HEAD_DIM)
        v = v.reshape(bsz, l, N_KV_HEADS, HEAD_DIM)
        q = apply_rope(rms_norm(q, q_norm_g[i]), cos, sin)
        k = apply_rope(rms_norm(k, k_norm_g[i]), cos, sin)
        o = gqa_block_attention(q, k, v)

        merged = jnp.concatenate([rms_norm(s, ssm_out_g[i]), rms_norm(o, attn_out_g[i])], axis=-1)
        x = x + merged @ w_out[i]

        h2 = rms_norm(x, norm2_g[i])
        g = dwconv_centered(h2 @ w_gate[i], conv_w[i], conv_b[i])
        x = x + (jax.nn.silu(g) * (h2 @ w_up[i])) @ w_down[i]
    return rms_norm(x, final_norm_g)


def setup_inputs(seed: int = 0) -> dict:
    key = jax.random.key(seed)
    ks = jax.random.split(key, 32)
    G, P, H = SSM_GROUPS, SSM_STATE, SSM_GROUP

    def nrm(k, shape, scale):
        return jax.random.normal(k, shape, F32) * scale

    def gain(k, shape):
        return 1.0 + 0.02 * jax.random.normal(k, shape, F32)

    lam_im_base = math.pi * jnp.arange(P, dtype=F32)
    return {
        "x_prompt": jax.random.normal(ks[0], (BATCH, SEQ, D_MODEL), F32),
        "x_sample": jax.random.normal(ks[1], (DEC_BATCH, DEC_SEQ, D_MODEL), F32),
        "norm1_g": gain(ks[2], (DEPTH, D_MODEL)),
        "w_in": nrm(ks[3], (DEPTH, D_MODEL, IN_WIDTH), D_MODEL ** -0.5),
        "lam_re": -0.5 + 0.01 * jax.random.normal(ks[4], (DEPTH, 2, G, P), F32),
        "lam_im": lam_im_base + 0.01 * jax.random.normal(ks[5], (DEPTH, 2, G, P), F32),
        "log_dt": jax.random.uniform(ks[6], (DEPTH, 2, G), F32, math.log(DT_MIN), math.log(DT_MAX)),
        "b_re": nrm(ks[7], (DEPTH, 2, G, P, H), (2.0 * H) ** -0.5),
        "b_im": nrm(ks[8], (DEPTH, 2, G, P, H), (2.0 * H) ** -0.5),
        "c_re": nrm(ks[9], (DEPTH, 2, G, H, P), (2.0 * P) ** -0.5),
        "c_im": nrm(ks[10], (DEPTH, 2, G, H, P), (2.0 * P) ** -0.5),
        "d_skip": nrm(ks[11], (DEPTH, G, H), 1.0),
        "w_glu": nrm(ks[12], (DEPTH, SSM_WIDTH, SSM_WIDTH), SSM_WIDTH ** -0.5),
        "b_glu": nrm(ks[13], (DEPTH, SSM_WIDTH), 0.02),
        "q_norm_g": gain(ks[14], (DEPTH, HEAD_DIM)),
        "k_norm_g": gain(ks[15], (DEPTH, HEAD_DIM)),
        "ssm_out_g": gain(ks[16], (DEPTH, SSM_WIDTH)),
        "attn_out_g": gain(ks[17], (DEPTH, ATTN_WIDTH)),
        "w_out": nrm(ks[18], (DEPTH, MIX_WIDTH, D_MODEL), MIX_WIDTH ** -0.5),
        "norm2_g": gain(ks[19], (DEPTH, D_MODEL)),
        "w_gate": nrm(ks[20], (DEPTH, D_MODEL, D_FF), D_MODEL ** -0.5),
        "w_up": nrm(ks[21], (DEPTH, D_MODEL, D_FF), D_MODEL ** -0.5),
        "conv_w": nrm(ks[22], (DEPTH, CONV_W, D_FF), CONV_W ** -0.5),
        "conv_b": nrm(ks[23], (DEPTH, D_FF), 0.02),
        "w_down": nrm(ks[24], (DEPTH, D_FF, D_MODEL), D_FF ** -0.5),
        "final_norm_g": gain(ks[25], (D_MODEL,)),
    }


def reference(x_prompt, x_sample, norm1_g, w_in, lam_re, lam_im, log_dt, b_re, b_im, c_re, c_im,
              d_skip, w_glu, b_glu, q_norm_g, k_norm_g, ssm_out_g, attn_out_g, w_out,
              norm2_g, w_gate, w_up, conv_w, conv_b, w_down, final_norm_g):
    y_prompt = encoder(x_prompt, norm1_g, w_in, lam_re, lam_im, log_dt, b_re, b_im, c_re, c_im,
                       d_skip, w_glu, b_glu, q_norm_g, k_norm_g, ssm_out_g, attn_out_g, w_out,
                       norm2_g, w_gate, w_up, conv_w, conv_b, w_down, final_norm_g)
    y_sample = encoder(x_sample, norm1_g, w_in, lam_re, lam_im, log_dt, b_re, b_im, c_re, c_im,
                       d_skip, w_glu, b_glu, q_norm_g, k_norm_g, ssm_out_g, attn_out_g, w_out,
                       norm2_g, w_gate, w_up, conv_w, conv_b, w_down, final_norm_g)
    return (y_prompt, y_sample)
```

```python
import functools
import math

import jax
import jax.numpy as jnp
from jax import lax
from jax.experimental import pallas as pl
from jax.experimental.pallas import tpu as pltpu

F32 = jnp.float32
BF16 = jnp.bfloat16

D_MODEL = 1024
SSM_WIDTH = 512
SSM_GROUP = 16
SSM_GROUPS = 32
SSM_STATE = 64
N_HEADS = 8
N_KV_HEADS = 2
HEAD_DIM = 64
ATTN_WIDTH = 512
KV_WIDTH = 128
IN_WIDTH = 1280
D_FF = 2816
GRID_W = 64
ROPE_THETA = 10000.0
EPS = 1e-6

LANES = 128
CHUNK = 16
CHUNK_W = CHUNK * SSM_GROUP
QK_WIDTH = ATTN_WIDTH + KV_WIDTH
SSM_ROW_BLOCK = 1024
FF_CHUNK = 256
HALO = 16
VMEM_LIMIT = 56 * 1024 * 1024


def _cparams(sem):
    return pltpu.CompilerParams(dimension_semantics=sem, vmem_limit_bytes=VMEM_LIMIT)


def _const_spec(shape):
    nd = len(shape)
    return pl.BlockSpec(shape, lambda *_: (0,) * nd, pipeline_mode=pl.Buffered(1))


def _rms(x, g):
    return x * lax.rsqrt(jnp.mean(x * x, axis=-1, keepdims=True) + EPS) * g


def _in_proj_kernel(x_ref, g1_ref, w_ref, ones_ref, qkg_ref, cos_ref, sin_ref,
                    u_ref, q_ref, k_ref, vt_ref):
    x = x_ref[0]
    h = _rms(x, g1_ref[...]).astype(BF16)
    z = jnp.dot(h, w_ref[...], preferred_element_type=F32)
    u_ref[0] = z[:, :SSM_WIDTH].astype(BF16)

    zqk = z[:, SSM_WIDTH:SSM_WIDTH + QK_WIDTH]
    sq = zqk * zqk
    sq_hi = sq.astype(BF16)
    sq_lo = (sq - sq_hi.astype(F32)).astype(BF16)
    ones = ones_ref[...]
    ss = (jnp.dot(sq_hi, ones, preferred_element_type=F32)
          + jnp.dot(sq_lo, ones, preferred_element_type=F32))
    y = zqk * lax.rsqrt(ss * (1.0 / HEAD_DIM) + EPS) * qkg_ref[...]

    tm = x.shape[0]
    lane = lax.broadcasted_iota(jnp.int32, (tm, LANES), 1)
    first_half = (lane % HEAD_DIM) < (HEAD_DIM // 2)
    c = cos_ref[...]
    s = sin_ref[...]
    for j in range(QK_WIDTH // LANES):
        yj = y[:, j * LANES:(j + 1) * LANES]
        partner = jnp.where(first_half,
                            pltpu.roll(yj, LANES - HEAD_DIM // 2, 1),
                            pltpu.roll(yj, HEAD_DIM // 2, 1))
        r = yj * c + partner * s
        if j < ATTN_WIDTH // LANES:
            q_ref[0, :, j * LANES:(j + 1) * LANES] = (r * (HEAD_DIM ** -0.5)).astype(BF16)
        else:
            low = lane < HEAD_DIM
            k_lo = jnp.where(low, r, 0.0)
            k_hi = jnp.where(low, 0.0, r)
            k_ref[0, :, 0 * LANES:1 * LANES] = k_lo.astype(BF16)
            k_ref[0, :, 1 * LANES:2 * LANES] = pltpu.roll(k_lo, HEAD_DIM, 1).astype(BF16)
            k_ref[0, :, 2 * LANES:3 * LANES] = pltpu.roll(k_hi, HEAD_DIM, 1).astype(BF16)
            k_ref[0, :, 3 * LANES:4 * LANES] = k_hi.astype(BF16)

    v = z[:, SSM_WIDTH + QK_WIDTH:]
    vt_ref[0] = jnp.transpose(v).astype(BF16)


def _in_proj(x, g1, w_in, ones, qkg, cos_t, sin_t, tm):
    b, l, _ = x.shape
    nl = l // tm
    return pl.pallas_call(
        _in_proj_kernel,
        grid=(b, nl),
        in_specs=[
            pl.BlockSpec((1, tm, D_MODEL), lambda i, j: (i, j, 0)),
            _const_spec((1, D_MODEL)),
            _const_spec((D_MODEL, IN_WIDTH)),
            _const_spec((QK_WIDTH, QK_WIDTH)),
            _const_spec((1, QK_WIDTH)),
            pl.BlockSpec((tm, LANES), lambda i, j: (j, 0)),
            pl.BlockSpec((tm, LANES), lambda i, j: (j, 0)),
        ],
        out_specs=[
            pl.BlockSpec((1, tm, SSM_WIDTH), lambda i, j: (i, j, 0)),
            pl.BlockSpec((1, tm, ATTN_WIDTH), lambda i, j: (i, j, 0)),
            pl.BlockSpec((1, tm, 4 * LANES), lambda i, j: (i, j, 0)),
            pl.BlockSpec((1, KV_WIDTH, tm), lambda i, j: (i, 0, j)),
        ],
        out_shape=[
            jax.ShapeDtypeStruct((b, l, SSM_WIDTH), BF16),
            jax.ShapeDtypeStruct((b, l, ATTN_WIDTH), BF16),
            jax.ShapeDtypeStruct((b, l, 4 * LANES), BF16),
            jax.ShapeDtypeStruct((b, KV_WIDTH, l), BF16),
        ],
        compiler_params=_cparams(("parallel", "parallel")),
        name="in_proj",
    )(x, g1, w_in, ones, qkg, cos_t, sin_t)


def _ssm_kernel(u_ref, t_ref, w_ref, v_ref, a_ref, y_ref, z_ref, sp_ref, e_ref, s_ref, *, nb, nc, chain):
    r = z_ref.shape[0]
    rb = min(r, SSM_ROW_BLOCK)

    def row_block(j):
        return pl.ds(pl.multiple_of(j * rb, rb), rb)

    def inject(j, carry):
        rows = row_block(j)
        z_ref[rows, :] = (jnp.dot(u_ref[0, rows, :], w_ref[0, :CHUNK_W, :], preferred_element_type=F32)
                          + jnp.dot(u_ref[1, rows, :], w_ref[0, CHUNK_W:, :], preferred_element_type=F32))
        return carry

    lax.fori_loop(0, r // rb, inject, 0)

    a = a_ref[0]
    afr = jnp.broadcast_to(a[0:1], (nb, LANES))
    afi = jnp.broadcast_to(a[1:2], (nb, LANES))
    abr = jnp.broadcast_to(a[2:3], (nb, LANES))
    abi = jnp.broadcast_to(a[3:4], (nb, LANES))

    def advance(i, carry, store):
        fr, fi, br, bi = carry
        row_f = pl.multiple_of(i * nb, 8)
        row_b = pl.multiple_of((nc - 1 - i) * nb, 8)
        if store:
            sp_ref[pl.ds(row_f, nb), 0 * LANES:1 * LANES] = fr
            sp_ref[pl.ds(row_f, nb), 1 * LANES:2 * LANES] = fi
            sp_ref[pl.ds(row_b, nb), 2 * LANES:3 * LANES] = br
            sp_ref[pl.ds(row_b, nb), 3 * LANES:4 * LANES] = bi
        zfr = z_ref[pl.ds(row_f, nb), 0 * LANES:1 * LANES]
        zfi = z_ref[pl.ds(row_f, nb), 1 * LANES:2 * LANES]
        zbr = z_ref[pl.ds(row_b, nb), 2 * LANES:3 * LANES]
        zbi = z_ref[pl.ds(row_b, nb), 3 * LANES:4 * LANES]
        return (afr * fr - afi * fi + zfr, afr * fi + afi * fr + zfi,
                abr * br - abi * bi + zbr, abr * bi + abi * br + zbi)

    zero = jnp.zeros((nb, LANES), F32)
    init = (zero, zero, zero, zero)
    if chain > 1:
        ends = lax.fori_loop(0, nc, functools.partial(advance, store=False), init)
        for k in range(4):
            e_ref[k] = ends[k]
        for base in range(0, nb, chain):
            for comp, order in ((0, range(base, base + chain)), (2, range(base + chain - 1, base - 1, -1))):
                sr = a[4 + comp:5 + comp]
                si = a[5 + comp:6 + comp]
                cr = jnp.zeros((1, LANES), F32)
                ci = jnp.zeros((1, LANES), F32)
                for row in order:
                    s_ref[comp, row:row + 1, :] = cr
                    s_ref[comp + 1, row:row + 1, :] = ci
                    er = e_ref[comp, row:row + 1, :]
                    ei = e_ref[comp + 1, row:row + 1, :]
                    cr, ci = sr * cr - si * ci + er, sr * ci + si * cr + ei
        init = (s_ref[0], s_ref[1], s_ref[2], s_ref[3])
    lax.fori_loop(0, nc, functools.partial(advance, store=True), init)

    def readout(j, carry):
        rows = row_block(j)
        yv = jnp.dot(sp_ref[rows, :].astype(BF16), v_ref[0], preferred_element_type=F32)
        for k in range(2):
            yk = jnp.dot(u_ref[k, rows, :], t_ref[k], preferred_element_type=F32)
            y_ref[k, rows, :] = (yk + yv[:, k * CHUNK_W:(k + 1) * CHUNK_W]).astype(y_ref.dtype)
        return carry

    lax.fori_loop(0, r // rb, readout, 0)


def _ssm(ug, t_mat, w_mat, v_mat, a_mat, nb, nc, chain):
    g, r, _ = ug.shape
    kern = functools.partial(_ssm_kernel, nb=nb, nc=nc, chain=chain)
    return pl.pallas_call(
        kern,
        grid=(g // 2,),
        in_specs=[
            pl.BlockSpec((2, r, CHUNK_W), lambda i: (i, 0, 0)),
            pl.BlockSpec((2, CHUNK_W, CHUNK_W), lambda i: (i, 0, 0)),
            pl.BlockSpec((1, 2 * CHUNK_W, 2 * CHUNK_W), lambda i: (i, 0, 0)),
            pl.BlockSpec((1, 2 * CHUNK_W, 2 * CHUNK_W), lambda i: (i, 0, 0)),
            pl.BlockSpec((1, 8, LANES), lambda i: (i, 0, 0)),
        ],
        out_specs=pl.BlockSpec((2, r, CHUNK_W), lambda i: (i, 0, 0)),
        out_shape=jax.ShapeDtypeStruct((g, r, CHUNK_W), BF16),
        scratch_shapes=[pltpu.VMEM((r, 2 * CHUNK_W), F32), pltpu.VMEM((r, 2 * CHUNK_W), F32),
                        pltpu.VMEM((4, nb, LANES), F32), pltpu.VMEM((4, nb, LANES), F32)],
        compiler_params=_cparams(("parallel",)),
        name="ssm",
    )(ug, t_mat, w_mat, v_mat, a_mat)


def _ssm_matrices(lam_re, lam_im, log_dt, b_re, b_im, c_re, c_im, d_skip):
    g, p, hh = SSM_GROUPS, SSM_STATE, SSM_GROUP
    dt = jnp.exp(log_dt.astype(F32))[..., None]
    lr = lam_re.astype(F32)
    li = lam_im.astype(F32)
    mag = jnp.exp(lr * dt)
    ab_re = mag * jnp.cos(li * dt)
    ab_im = mag * jnp.sin(li * dt)
    den = lr * lr + li * li
    nr = ab_re - 1.0
    ni = ab_im
    zr = (nr * lr + ni * li) / den
    zi = (ni * lr - nr * li) / den
    br = b_re.astype(F32)
    bi = b_im.astype(F32)
    bb_re = zr[..., None] * br - zi[..., None] * bi
    bb_im = zr[..., None] * bi + zi[..., None] * br
    cr = c_re.astype(F32)
    ci = c_im.astype(F32)

    ks = jnp.arange(CHUNK + 1, dtype=F32)[:, None, None, None]
    pmag = jnp.exp(lr * dt * ks)
    pw_re = pmag * jnp.cos(li * dt * ks)
    pw_im = pmag * jnp.sin(li * dt * ks)

    cp_re = cr[None] * pw_re[:CHUNK, :, :, None, :] - ci[None] * pw_im[:CHUNK, :, :, None, :]
    cp_im = cr[None] * pw_im[:CHUNK, :, :, None, :] + ci[None] * pw_re[:CHUNK, :, :, None, :]
    kern = (jnp.einsum('kdgop,dgph->kdgoh', cp_re, bb_re, precision=lax.Precision.HIGHEST)
            - jnp.einsum('kdgop,dgph->kdgoh', cp_im, bb_im, precision=lax.Precision.HIGHEST))

    s_idx = jnp.arange(CHUNK)[:, None]
    t_idx = jnp.arange(CHUNK)[None, :]
    lag = t_idx - s_idx
    kf = kern[:, 0]
    kb = kern[:, 1]
    tf = jnp.where((lag >= 0)[..., None, None, None], kf[jnp.clip(lag, 0, CHUNK - 1)], 0.0)
    tb = jnp.where((lag <= 0)[..., None, None, None], kb[jnp.clip(-lag, 0, CHUNK - 1)], 0.0)
    tt = tf + tb
    eye_h = jnp.eye(hh, dtype=F32)
    skip = (lag == 0)[..., None, None, None] * (d_skip.astype(F32)[None, None, :, :, None] * eye_h)
    tt = tt + skip
    t_mat = jnp.transpose(tt, (2, 0, 4, 1, 3)).reshape(g, CHUNK_W, CHUNK_W)

    pf_re = pw_re[:CHUNK, 0][::-1]
    pf_im = pw_im[:CHUNK, 0][::-1]
    pb_re = pw_re[:CHUNK, 1]
    pb_im = pw_im[:CHUNK, 1]

    def inj(p_re, p_im, d):
        w_re = p_re[..., None] * bb_re[d][None] - p_im[..., None] * bb_im[d][None]
        w_im = p_re[..., None] * bb_im[d][None] + p_im[..., None] * bb_re[d][None]
        w_re = jnp.transpose(w_re, (1, 0, 3, 2)).reshape(g, CHUNK_W, p)
        w_im = jnp.transpose(w_im, (1, 0, 3, 2)).reshape(g, CHUNK_W, p)
        return w_re, w_im

    wf_re, wf_im = inj(pf_re, pf_im, 0)
    wb_re, wb_im = inj(pb_re, pb_im, 1)

    def pair_cols(m):
        m = m.reshape(g // 2, 2, CHUNK_W, p)
        zeros = jnp.zeros_like(m[:, 0])
        top = jnp.concatenate([m[:, 0], zeros], axis=-1)
        bot = jnp.concatenate([zeros, m[:, 1]], axis=-1)
        return jnp.concatenate([top, bot], axis=1)

    w_mat = jnp.concatenate([pair_cols(wf_re), pair_cols(wf_im), pair_cols(wb_re), pair_cols(wb_im)], axis=-1)

    def readout(q_re, q_im, d):
        v_re = cr[d][None] * q_re[:, :, None, :] - ci[d][None] * q_im[:, :, None, :]
        v_im = cr[d][None] * q_im[:, :, None, :] + ci[d][None] * q_re[:, :, None, :]
        v_re = jnp.transpose(v_re, (1, 3, 0, 2)).reshape(g, p, CHUNK_W)
        v_im = jnp.transpose(v_im, (1, 3, 0, 2)).reshape(g, p, CHUNK_W)
        return v_re, -v_im

    vf_re, vf_im = readout(pw_re[1:CHUNK + 1, 0], pw_im[1:CHUNK + 1, 0], 0)
    vb_re, vb_im = readout(pw_re[1:CHUNK + 1, 1][::-1], pw_im[1:CHUNK + 1, 1][::-1], 1)

    def pair_rows(m):
        m = m.reshape(g // 2, 2, p, CHUNK_W)
        zeros = jnp.zeros_like(m[:, 0])
        top = jnp.concatenate([m[:, 0], zeros], axis=-1)
        bot = jnp.concatenate([zeros, m[:, 1]], axis=-1)
        return jnp.concatenate([top, bot], axis=1)

    v_mat = jnp.concatenate([pair_rows(vf_re), pair_rows(vf_im), pair_rows(vb_re), pair_rows(vb_im)], axis=1)

    return t_mat.astype(BF16), w_mat.astype(BF16), v_mat.astype(BF16), (lr * dt, li * dt)


def _ssm_decay_rows(lam_dt, seg_tokens):
    lr_dt, li_dt = lam_dt
    rows = []
    for n in (float(CHUNK), float(seg_tokens)):
        mag = jnp.exp(lr_dt * n)
        for d in range(2):
            rows.append((mag[d] * jnp.cos(li_dt[d] * n)).reshape(SSM_GROUPS // 2, 2 * SSM_STATE))
            rows.append((mag[d] * jnp.sin(li_dt[d] * n)).reshape(SSM_GROUPS // 2, 2 * SSM_STATE))
    return jnp.stack(rows, axis=1)


def _attn_kernel(q_ref, ka_ref, kb_ref, vt_ref, o_ref, m_ref, l_ref, acc_ref, *, tk, nk):
    q2 = q_ref[0]
    tq = q2.shape[0]
    m_ref[...] = jnp.full(m_ref.shape, -jnp.inf, F32)
    l_ref[...] = jnp.zeros(l_ref.shape, F32)
    acc_ref[...] = jnp.zeros(acc_ref.shape, F32)

    def body(i, carry):
        start = pl.multiple_of(i * tk, tk)
        vt = vt_ref[0, :, pl.ds(start, tk)]
        for hh, k_ref in enumerate((ka_ref, kb_ref)):
            kk = k_ref[0, pl.ds(start, tk), :]
            st = lax.dot_general(kk, q2, (((1,), (1,)), ((), ())),
                                 preferred_element_type=F32)
            m_old = m_ref[hh]
            m_new = jnp.maximum(m_old, jnp.max(st, axis=0, keepdims=True))
            alpha = jnp.exp(m_old - m_new)
            p = jnp.exp(st - m_new)
            l_ref[hh] = alpha * l_ref[hh] + jnp.sum(p, axis=0, keepdims=True)
            acc_ref[hh] = alpha * acc_ref[hh] + jnp.dot(vt, p.astype(BF16),
                                                        preferred_element_type=F32)
            m_ref[hh] = m_new
        return carry

    lax.fori_loop(0, nk, body, 0)
    o = jnp.concatenate([acc_ref[0] / l_ref[0], acc_ref[1] / l_ref[1]], axis=0)
    o_ref[0] = jnp.transpose(o).astype(o_ref.dtype)


def _attention(q, kpad, vt, tq, tk):
    b, l, _ = q.shape
    nq = l // tq
    nk = l // tk
    kern = functools.partial(_attn_kernel, tk=tk, nk=nk)
    return pl.pallas_call(
        kern,
        grid=(b, ATTN_WIDTH // LANES, nq),
        in_specs=[
            pl.BlockSpec((1, tq, LANES), lambda i, j, n: (i, n, j)),
            pl.BlockSpec((1, l, LANES), lambda i, j, n: (i, 0, 2 * (j // 2))),
            pl.BlockSpec((1, l, LANES), lambda i, j, n: (i, 0, 2 * (j // 2) + 1)),
            pl.BlockSpec((1, HEAD_DIM, l), lambda i, j, n: (i, j // 2, 0)),
        ],
        out_specs=pl.BlockSpec((1, tq, LANES), lambda i, j, n: (i, n, j)),
        out_shape=jax.ShapeDtypeStruct((b, l, ATTN_WIDTH), BF16),
        scratch_shapes=[pltpu.VMEM((2, 1, tq), F32), pltpu.VMEM((2, 1, tq), F32),
                        pltpu.VMEM((2, HEAD_DIM, tq), F32)],
        compiler_params=_cparams(("parallel", "parallel", "parallel")),
        name="attention",
    )(q, kpad, kpad, vt)


def _mix_out_kernel(x_ref, ys_ref, o_ref, wglu_ref, bglu_ref, gs_ref, go_ref, wos_ref, woo_ref, g2_ref,
                    x1_ref, h2_ref):
    s = jax.nn.gelu(ys_ref[0].astype(F32))
    gate = jnp.dot(s.astype(BF16), wglu_ref[...], preferred_element_type=F32) + bglu_ref[...]
    s = s * jax.nn.sigmoid(gate)
    sn = _rms(s, gs_ref[...]).astype(BF16)
    on = _rms(o_ref[0].astype(F32), go_ref[...]).astype(BF16)
    x1 = (x_ref[0]
          + jnp.dot(sn, wos_ref[...], preferred_element_type=F32)
          + jnp.dot(on, woo_ref[...], preferred_element_type=F32))
    x1_ref[0] = x1
    h2_ref[0] = _rms(x1, g2_ref[...]).astype(BF16)


def _mix_out(x, ys, o, w_glu, b_glu, gs, go, wo_s, wo_o, g2, tm):
    b, l, _ = x.shape
    tok = lambda w: pl.BlockSpec((1, tm, w), lambda i, j: (i, j, 0))
    return pl.pallas_call(
        _mix_out_kernel,
        grid=(b, l // tm),
        in_specs=[
            tok(D_MODEL), tok(SSM_WIDTH), tok(ATTN_WIDTH),
            _const_spec((SSM_WIDTH, SSM_WIDTH)), _const_spec((1, SSM_WIDTH)),
            _const_spec((1, SSM_WIDTH)), _const_spec((1, ATTN_WIDTH)),
            _const_spec((SSM_WIDTH, D_MODEL)), _const_spec((ATTN_WIDTH, D_MODEL)),
            _const_spec((1, D_MODEL)),
        ],
        out_specs=[tok(D_MODEL), tok(D_MODEL)],
        out_shape=[jax.ShapeDtypeStruct((b, l, D_MODEL), F32),
                   jax.ShapeDtypeStruct((b, l, D_MODEL), BF16)],
        compiler_params=_cparams(("parallel", "parallel")),
        name="mix_out",
    )(x, ys, o, w_glu, b_glu, gs, go, wo_s, wo_o, g2)


def _ffn_kernel(h2_ref, hp_ref, hn_ref, x1_ref, wg_ref, wu_ref, cw_ref, cb_ref, wd_ref, gf_ref,
                y_ref, acc_ref):
    i = pl.program_id(1)
    last = pl.num_programs(1) - 1
    h2 = h2_ref[0]
    tm = h2.shape[0]
    hcat = jnp.concatenate([hp_ref[0], h2, hn_ref[0]], axis=0)
    rows = tm + 2 * HALO
    row = lax.broadcasted_iota(jnp.int32, (tm, 1), 0)
    keep_prev = jnp.logical_not(jnp.logical_and(row == 0, i == 0))
    keep_next = jnp.logical_not(jnp.logical_and(row == tm - 1, i == last))
    acc_ref[...] = x1_ref[0]
    for c in range(D_FF // FF_CHUNK):
        cs = slice(c * FF_CHUNK, (c + 1) * FF_CHUNK)
        g = jnp.dot(hcat, wg_ref[:, cs], preferred_element_type=F32)
        g_prev = jnp.where(keep_prev, pltpu.roll(g, 1, 0)[HALO:HALO + tm], 0.0)
        g_next = jnp.where(keep_next, pltpu.roll(g, rows - 1, 0)[HALO:HALO + tm], 0.0)
        conv = (cb_ref[:, cs] + g_prev * cw_ref[0:1, cs] + g[HALO:HALO + tm] * cw_ref[1:2, cs]
                + g_next * cw_ref[2:3, cs])
        up = jnp.dot(h2, wu_ref[:, cs], preferred_element_type=F32)
        act = (jax.nn.silu(conv) * up).astype(BF16)
        acc_ref[...] += jnp.dot(act, wd_ref[cs, :], preferred_element_type=F32)
    y_ref[0] = _rms(acc_ref[...], gf_ref[...])


def _ffn(h2, x1, w_gate, w_up, conv_w, conv_b, w_down, gf, tm):
    b, l, _ = x1.shape
    nh = tm // HALO
    nblk = l // HALO
    tok = lambda: pl.BlockSpec((1, tm, D_MODEL), lambda i, j: (i, j, 0))
    return pl.pallas_call(
        _ffn_kernel,
        grid=(b, l // tm),
        in_specs=[
            tok(),
            pl.BlockSpec((1, HALO, D_MODEL), lambda i, j: (i, jnp.maximum(j * nh - 1, 0), 0)),
            pl.BlockSpec((1, HALO, D_MODEL), lambda i, j: (i, jnp.minimum((j + 1) * nh, nblk - 1), 0)),
            tok(),
            _const_spec((D_MODEL, D_FF)), _const_spec((D_MODEL, D_FF)),
            _const_spec((8, D_FF)), _const_spec((1, D_FF)),
            _const_spec((D_FF, D_MODEL)), _const_spec((1, D_MODEL)),
        ],
        out_specs=tok(),
        out_shape=jax.ShapeDtypeStruct((b, l, D_MODEL), F32),
        scratch_shapes=[pltpu.VMEM((tm, D_MODEL), F32)],
        compiler_params=_cparams(("parallel", "parallel")),
        name="ffn",
    )(h2, h2, h2, x1, w_gate, w_up, conv_w, conv_b, w_down, gf)


def _rope_tables(seq_len):
    rows = seq_len // GRID_W
    row_id = jnp.broadcast_to(jnp.arange(rows)[:, None], (rows, GRID_W)).reshape(-1).astype(F32)
    col_id = jnp.broadcast_to(jnp.arange(GRID_W)[None, :], (rows, GRID_W)).reshape(-1).astype(F32)
    axis_dim = HEAD_DIM // 2
    inv_freq = ROPE_THETA ** (-jnp.arange(0, axis_dim, 2, dtype=F32) / axis_dim)
    ang = jnp.concatenate([row_id[:, None] * inv_freq, col_id[:, None] * inv_freq], axis=-1)
    cos, sin = jnp.cos(ang), jnp.sin(ang)
    cos_t = jnp.tile(cos, (1, LANES // (HEAD_DIM // 2)))
    sin_t = jnp.tile(jnp.concatenate([-sin, sin], axis=-1), (1, LANES // HEAD_DIM))
    return cos_t, sin_t


def _encoder(x, p, tm, tq, tk, tm_ffn):
    b, l, _ = x.shape
    cos_t, sin_t = _rope_tables(l)
    u, q, kpad, vt = _in_proj(x, p["g1"], p["w_in"], p["ones"], p["qkg"], cos_t, sin_t, tm)

    chain = 1 if b % 8 == 0 else 8 // b
    nb = b * chain
    nc = l // (chain * CHUNK)
    a_mat = _ssm_decay_rows(p["lam_dt"], nc * CHUNK)
    ug = u.reshape(nb, nc, CHUNK, SSM_GROUPS, SSM_GROUP)
    ug = jnp.transpose(ug, (3, 1, 0, 2, 4)).reshape(SSM_GROUPS, nc * nb, CHUNK_W)
    yg = _ssm(ug, p["t_mat"], p["w_mat"], p["v_mat"], a_mat, nb, nc, chain)
    ys = jnp.transpose(yg.reshape(SSM_GROUPS, nc, nb, CHUNK, SSM_GROUP), (2, 1, 3, 0, 4)).reshape(b, l, SSM_WIDTH)

    o = _attention(q, kpad, vt, tq, tk)
    x1, h2 = _mix_out(x, ys, o, p["w_glu"], p["b_glu"], p["gs"], p["go"], p["wo_s"], p["wo_o"], p["g2"], tm)
    return _ffn(h2, x1, p["w_gate"], p["w_up"], p["conv_w"], p["conv_b"], p["w_down"], p["gf"], tm_ffn)


def _prepare(norm1_g, w_in, lam_re, lam_im, log_dt, b_re, b_im, c_re, c_im, d_skip, w_glu, b_glu,
             q_norm_g, k_norm_g, ssm_out_g, attn_out_g, w_out, norm2_g, w_gate, w_up, conv_w, conv_b,
             w_down, final_norm_g):
    i = 0
    t_mat, w_mat, v_mat, lam_dt = _ssm_matrices(lam_re[i], lam_im[i], log_dt[i], b_re[i], b_im[i],
                                                c_re[i], c_im[i], d_skip[i])
    head = jnp.arange(QK_WIDTH) // HEAD_DIM
    ones = (head[:, None] == head[None, :]).astype(BF16)
    qkg = jnp.concatenate([jnp.tile(q_norm_g[i], N_HEADS), jnp.tile(k_norm_g[i], N_KV_HEADS)])[None].astype(F32)
    row = lambda v: v[None].astype(F32)
    return dict(
        g1=row(norm1_g[i]), w_in=w_in[i].astype(BF16), ones=ones, qkg=qkg,
        t_mat=t_mat, w_mat=w_mat, v_mat=v_mat, lam_dt=lam_dt,
        w_glu=w_glu[i].astype(BF16), b_glu=row(b_glu[i]), gs=row(ssm_out_g[i]), go=row(attn_out_g[i]),
        wo_s=w_out[i, :SSM_WIDTH].astype(BF16), wo_o=w_out[i, SSM_WIDTH:].astype(BF16), g2=row(norm2_g[i]),
        w_gate=w_gate[i].astype(BF16), w_up=w_up[i].astype(BF16),
        conv_w=jnp.concatenate([conv_w[i].astype(F32), jnp.zeros((5, D_FF), F32)], axis=0),
        conv_b=row(conv_b[i]), w_down=w_down[i].astype(BF16), gf=row(final_norm_g),
    )


def kernel(x_prompt, x_sample, norm1_g, w_in, lam_re, lam_im, log_dt, b_re, b_im, c_re, c_im, d_skip, w_glu, b_glu, q_norm_g, k_norm_g, ssm_out_g, attn_out_g, w_out, norm2_g, w_gate, w_up, conv_w, conv_b, w_down, final_norm_g):
    p = _prepare(norm1_g, w_in, lam_re, lam_im, log_dt, b_re, b_im, c_re, c_im, d_skip, w_glu, b_glu,
                 q_norm_g, k_norm_g, ssm_out_g, attn_out_g, w_out, norm2_g, w_gate, w_up, conv_w, conv_b,
                 w_down, final_norm_g)
    y_prompt = _encoder(x_prompt, p, tm=512, tq=512, tk=512, tm_ffn=512)
    y_sample = _encoder(x_sample, p, tm=512, tq=512, tk=512, tm_ffn=512)
    return (y_prompt, y_sample)
```

```python
import functools
import math

import jax
import jax.numpy as jnp
from jax import lax
from jax.experimental import pallas as pl
from jax.experimental.pallas import tpu as pltpu

F32 = jnp.float32
BF16 = jnp.bfloat16

D_MODEL = 1024
SSM_WIDTH = 512
SSM_GROUP = 16
SSM_GROUPS = 32
SSM_STATE = 64
N_HEADS = 8
N_KV_HEADS = 2
HEAD_DIM = 64
ATTN_WIDTH = 512
KV_WIDTH = 128
IN_WIDTH = 1280
D_FF = 2816
GRID_W = 64
ROPE_THETA = 10000.0
EPS = 1e-6

LANES = 128
CHUNK = 16
CHUNK_W = CHUNK * SSM_GROUP
QK_WIDTH = ATTN_WIDTH + KV_WIDTH
SSM_ROW_BLOCK = 1024
FF_CHUNK = 256
HALO = 16
ONES_ROWS = 16
Q_SCALE = HEAD_DIM ** -0.5 * math.log2(math.e)
VMEM_LIMIT = 56 * 1024 * 1024


def _cparams(sem):
    return pltpu.CompilerParams(dimension_semantics=sem, vmem_limit_bytes=VMEM_LIMIT)


def _const_spec(shape):
    nd = len(shape)
    return pl.BlockSpec(shape, lambda *_: (0,) * nd, pipeline_mode=pl.Buffered(1))


def _rms(x, g):
    return x * lax.rsqrt(jnp.mean(x * x, axis=-1, keepdims=True) + EPS) * g


def _in_proj_kernel(x_ref, g1_ref, w_ref, ones_ref, qkg_ref, cos_ref, sin_ref,
                    u_ref, q_ref, k_ref, vt_ref):
    x = x_ref[0]
    h = _rms(x, g1_ref[...]).astype(BF16)
    z = jnp.dot(h, w_ref[...], preferred_element_type=F32)
    u_ref[0] = z[:, :SSM_WIDTH].astype(BF16)

    zqk = z[:, SSM_WIDTH:SSM_WIDTH + QK_WIDTH]
    sq = zqk * zqk
    sq_hi = sq.astype(BF16)
    sq_lo = (sq - sq_hi.astype(F32)).astype(BF16)
    ones = ones_ref[...]
    ss = (jnp.dot(sq_hi, ones, preferred_element_type=F32)
          + jnp.dot(sq_lo, ones, preferred_element_type=F32))
    y = zqk * lax.rsqrt(ss * (1.0 / HEAD_DIM) + EPS) * qkg_ref[...]

    tm = x.shape[0]
    lane = lax.broadcasted_iota(jnp.int32, (tm, LANES), 1)
    first_half = (lane % HEAD_DIM) < (HEAD_DIM // 2)
    c = cos_ref[...]
    s = sin_ref[...]
    for j in range(QK_WIDTH // LANES):
        yj = y[:, j * LANES:(j + 1) * LANES]
        partner = jnp.where(first_half,
                            pltpu.roll(yj, LANES - HEAD_DIM // 2, 1),
                            pltpu.roll(yj, HEAD_DIM // 2, 1))
        r = yj * c + partner * s
        if j < ATTN_WIDTH // LANES:
            q_ref[0, :, j * LANES:(j + 1) * LANES] = (r * Q_SCALE).astype(BF16)
        else:
            low = lane < HEAD_DIM
            k_lo = jnp.where(low, r, 0.0)
            k_hi = jnp.where(low, 0.0, r)
            k_ref[0, :, 0 * LANES:1 * LANES] = k_lo.astype(BF16)
            k_ref[0, :, 1 * LANES:2 * LANES] = pltpu.roll(k_lo, HEAD_DIM, 1).astype(BF16)
            k_ref[0, :, 2 * LANES:3 * LANES] = pltpu.roll(k_hi, HEAD_DIM, 1).astype(BF16)
            k_ref[0, :, 3 * LANES:4 * LANES] = k_hi.astype(BF16)

    v = z[:, SSM_WIDTH + QK_WIDTH:]
    vt_ref[0] = jnp.transpose(v).astype(BF16)


def _in_proj(x, g1, w_in, ones, qkg, cos_t, sin_t, tm):
    b, l, _ = x.shape
    nl = l // tm
    return pl.pallas_call(
        _in_proj_kernel,
        grid=(b, nl),
        in_specs=[
            pl.BlockSpec((1, tm, D_MODEL), lambda i, j: (i, j, 0)),
            _const_spec((1, D_MODEL)),
            _const_spec((D_MODEL, IN_WIDTH)),
            _const_spec((QK_WIDTH, QK_WIDTH)),
            _const_spec((1, QK_WIDTH)),
            pl.BlockSpec((tm, LANES), lambda i, j: (j, 0)),
            pl.BlockSpec((tm, LANES), lambda i, j: (j, 0)),
        ],
        out_specs=[
            pl.BlockSpec((1, tm, SSM_WIDTH), lambda i, j: (i, j, 0)),
            pl.BlockSpec((1, tm, ATTN_WIDTH), lambda i, j: (i, j, 0)),
            pl.BlockSpec((1, tm, 4 * LANES), lambda i, j: (i, j, 0)),
            pl.BlockSpec((1, KV_WIDTH, tm), lambda i, j: (i, 0, j)),
        ],
        out_shape=[
            jax.ShapeDtypeStruct((b, l, SSM_WIDTH), BF16),
            jax.ShapeDtypeStruct((b, l, ATTN_WIDTH), BF16),
            jax.ShapeDtypeStruct((b, l, 4 * LANES), BF16),
            jax.ShapeDtypeStruct((b, KV_WIDTH, l), BF16),
        ],
        compiler_params=_cparams(("parallel", "parallel")),
        name="in_proj",
    )(x, g1, w_in, ones, qkg, cos_t, sin_t)


def _ssm_kernel(u_ref, t_ref, w_ref, v_ref, a_ref, y_ref, z_ref, sp_ref, e_ref, s_ref, *, nb, nc, chain):
    r = z_ref.shape[0]
    rb = min(r, SSM_ROW_BLOCK)

    def row_block(j):
        return pl.ds(pl.multiple_of(j * rb, rb), rb)

    def inject(j, carry):
        rows = row_block(j)
        z_ref[rows, :] = (jnp.dot(u_ref[0, rows, :], w_ref[0, :CHUNK_W, :], preferred_element_type=F32)
                          + jnp.dot(u_ref[1, rows, :], w_ref[0, CHUNK_W:, :], preferred_element_type=F32))
        return carry

    lax.fori_loop(0, r // rb, inject, 0)

    a = a_ref[0]
    afr = jnp.broadcast_to(a[0:1], (nb, LANES))
    afi = jnp.broadcast_to(a[1:2], (nb, LANES))
    abr = jnp.broadcast_to(a[2:3], (nb, LANES))
    abi = jnp.broadcast_to(a[3:4], (nb, LANES))

    def advance(i, carry, store):
        fr, fi, br, bi = carry
        row_f = pl.multiple_of(i * nb, 8)
        row_b = pl.multiple_of((nc - 1 - i) * nb, 8)
        if store:
            sp_ref[pl.ds(row_f, nb), 0 * LANES:1 * LANES] = fr
            sp_ref[pl.ds(row_f, nb), 1 * LANES:2 * LANES] = fi
            sp_ref[pl.ds(row_b, nb), 2 * LANES:3 * LANES] = br
            sp_ref[pl.ds(row_b, nb), 3 * LANES:4 * LANES] = bi
        zfr = z_ref[pl.ds(row_f, nb), 0 * LANES:1 * LANES]
        zfi = z_ref[pl.ds(row_f, nb), 1 * LANES:2 * LANES]
        zbr = z_ref[pl.ds(row_b, nb), 2 * LANES:3 * LANES]
        zbi = z_ref[pl.ds(row_b, nb), 3 * LANES:4 * LANES]
        return (afr * fr - afi * fi + zfr, afr * fi + afi * fr + zfi,
                abr * br - abi * bi + zbr, abr * bi + abi * br + zbi)

    zero = jnp.zeros((nb, LANES), F32)
    init = (zero, zero, zero, zero)
    if chain > 1:
        ends = lax.fori_loop(0, nc, functools.partial(advance, store=False), init)
        for k in range(4):
            e_ref[k] = ends[k]
        for base in range(0, nb, chain):
            for comp, order in ((0, range(base, base + chain)), (2, range(base + chain - 1, base - 1, -1))):
                sr = a[4 + comp:5 + comp]
                si = a[5 + comp:6 + comp]
                cr = jnp.zeros((1, LANES), F32)
                ci = jnp.zeros((1, LANES), F32)
                for row in order:
                    s_ref[comp, row:row + 1, :] = cr
                    s_ref[comp + 1, row:row + 1, :] = ci
                    er = e_ref[comp, row:row + 1, :]
                    ei = e_ref[comp + 1, row:row + 1, :]
                    cr, ci = sr * cr - si * ci + er, sr * ci + si * cr + ei
        init = (s_ref[0], s_ref[1], s_ref[2], s_ref[3])
    lax.fori_loop(0, nc, functools.partial(advance, store=True), init)

    def readout(j, carry):
        rows = row_block(j)
        yv = jnp.dot(sp_ref[rows, :].astype(BF16), v_ref[0], preferred_element_type=F32)
        for k in range(2):
            yk = jnp.dot(u_ref[k, rows, :], t_ref[k], preferred_element_type=F32)
            y_ref[k, rows, :] = (yk + yv[:, k * CHUNK_W:(k + 1) * CHUNK_W]).astype(y_ref.dtype)
        return carry

    lax.fori_loop(0, r // rb, readout, 0)


def _ssm(ug, t_mat, w_mat, v_mat, a_mat, nb, nc, chain):
    g, r, _ = ug.shape
    kern = functools.partial(_ssm_kernel, nb=nb, nc=nc, chain=chain)
    return pl.pallas_call(
        kern,
        grid=(g // 2,),
        in_specs=[
            pl.BlockSpec((2, r, CHUNK_W), lambda i: (i, 0, 0)),
            pl.BlockSpec((2, CHUNK_W, CHUNK_W), lambda i: (i, 0, 0)),
            pl.BlockSpec((1, 2 * CHUNK_W, 2 * CHUNK_W), lambda i: (i, 0, 0)),
            pl.BlockSpec((1, 2 * CHUNK_W, 2 * CHUNK_W), lambda i: (i, 0, 0)),
            pl.BlockSpec((1, 8, LANES), lambda i: (i, 0, 0)),
        ],
        out_specs=pl.BlockSpec((2, r, CHUNK_W), lambda i: (i, 0, 0)),
        out_shape=jax.ShapeDtypeStruct((g, r, CHUNK_W), BF16),
        scratch_shapes=[pltpu.VMEM((r, 2 * CHUNK_W), F32), pltpu.VMEM((r, 2 * CHUNK_W), F32),
                        pltpu.VMEM((4, nb, LANES), F32), pltpu.VMEM((4, nb, LANES), F32)],
        compiler_params=_cparams(("parallel",)),
        name="ssm",
    )(ug, t_mat, w_mat, v_mat, a_mat)


def _ssm_matrices(lam_re, lam_im, log_dt, b_re, b_im, c_re, c_im, d_skip):
    g, p, hh = SSM_GROUPS, SSM_STATE, SSM_GROUP
    dt = jnp.exp(log_dt.astype(F32))[..., None]
    lr = lam_re.astype(F32)
    li = lam_im.astype(F32)
    mag = jnp.exp(lr * dt)
    ab_re = mag * jnp.cos(li * dt)
    ab_im = mag * jnp.sin(li * dt)
    den = lr * lr + li * li
    nr = ab_re - 1.0
    ni = ab_im
    zr = (nr * lr + ni * li) / den
    zi = (ni * lr - nr * li) / den
    br = b_re.astype(F32)
    bi = b_im.astype(F32)
    bb_re = zr[..., None] * br - zi[..., None] * bi
    bb_im = zr[..., None] * bi + zi[..., None] * br
    cr = c_re.astype(F32)
    ci = c_im.astype(F32)

    ks = jnp.arange(CHUNK + 1, dtype=F32)[:, None, None, None]
    pmag = jnp.exp(lr * dt * ks)
    pw_re = pmag * jnp.cos(li * dt * ks)
    pw_im = pmag * jnp.sin(li * dt * ks)

    cp_re = cr[None] * pw_re[:CHUNK, :, :, None, :] - ci[None] * pw_im[:CHUNK, :, :, None, :]
    cp_im = cr[None] * pw_im[:CHUNK, :, :, None, :] + ci[None] * pw_re[:CHUNK, :, :, None, :]
    kern = (jnp.einsum('kdgop,dgph->kdgoh', cp_re, bb_re, precision=lax.Precision.HIGHEST)
            - jnp.einsum('kdgop,dgph->kdgoh', cp_im, bb_im, precision=lax.Precision.HIGHEST))

    s_idx = jnp.arange(CHUNK)[:, None]
    t_idx = jnp.arange(CHUNK)[None, :]
    lag = t_idx - s_idx
    kf = kern[:, 0]
    kb = kern[:, 1]
    tf = jnp.where((lag >= 0)[..., None, None, None], kf[jnp.clip(lag, 0, CHUNK - 1)], 0.0)
    tb = jnp.where((lag <= 0)[..., None, None, None], kb[jnp.clip(-lag, 0, CHUNK - 1)], 0.0)
    tt = tf + tb
    eye_h = jnp.eye(hh, dtype=F32)
    skip = (lag == 0)[..., None, None, None] * (d_skip.astype(F32)[None, None, :, :, None] * eye_h)
    tt = tt + skip
    t_mat = jnp.transpose(tt, (2, 0, 4, 1, 3)).reshape(g, CHUNK_W, CHUNK_W)

    pf_re = pw_re[:CHUNK, 0][::-1]
    pf_im = pw_im[:CHUNK, 0][::-1]
    pb_re = pw_re[:CHUNK, 1]
    pb_im = pw_im[:CHUNK, 1]

    def inj(p_re, p_im, d):
        w_re = p_re[..., None] * bb_re[d][None] - p_im[..., None] * bb_im[d][None]
        w_im = p_re[..., None] * bb_im[d][None] + p_im[..., None] * bb_re[d][None]
        w_re = jnp.transpose(w_re, (1, 0, 3, 2)).reshape(g, CHUNK_W, p)
        w_im = jnp.transpose(w_im, (1, 0, 3, 2)).reshape(g, CHUNK_W, p)
        return w_re, w_im

    wf_re, wf_im = inj(pf_re, pf_im, 0)
    wb_re, wb_im = inj(pb_re, pb_im, 1)

    def pair_cols(m):
        m = m.reshape(g // 2, 2, CHUNK_W, p)
        zeros = jnp.zeros_like(m[:, 0])
        top = jnp.concatenate([m[:, 0], zeros], axis=-1)
        bot = jnp.concatenate([zeros, m[:, 1]], axis=-1)
        return jnp.concatenate([top, bot], axis=1)

    w_mat = jnp.concatenate([pair_cols(wf_re), pair_cols(wf_im), pair_cols(wb_re), pair_cols(wb_im)], axis=-1)

    def readout(q_re, q_im, d):
        v_re = cr[d][None] * q_re[:, :, None, :] - ci[d][None] * q_im[:, :, None, :]
        v_im = cr[d][None] * q_im[:, :, None, :] + ci[d][None] * q_re[:, :, None, :]
        v_re = jnp.transpose(v_re, (1, 3, 0, 2)).reshape(g, p, CHUNK_W)
        v_im = jnp.transpose(v_im, (1, 3, 0, 2)).reshape(g, p, CHUNK_W)
        return v_re, -v_im

    vf_re, vf_im = readout(pw_re[1:CHUNK + 1, 0], pw_im[1:CHUNK + 1, 0], 0)
    vb_re, vb_im = readout(pw_re[1:CHUNK + 1, 1][::-1], pw_im[1:CHUNK + 1, 1][::-1], 1)

    def pair_rows(m):
        m = m.reshape(g // 2, 2, p, CHUNK_W)
        zeros = jnp.zeros_like(m[:, 0])
        top = jnp.concatenate([m[:, 0], zeros], axis=-1)
        bot = jnp.concatenate([zeros, m[:, 1]], axis=-1)
        return jnp.concatenate([top, bot], axis=1)

    v_mat = jnp.concatenate([pair_rows(vf_re), pair_rows(vf_im), pair_rows(vb_re), pair_rows(vb_im)], axis=1)

    return t_mat.astype(BF16), w_mat.astype(BF16), v_mat.astype(BF16), (lr * dt, li * dt)


def _ssm_decay_rows(lam_dt, seg_tokens):
    lr_dt, li_dt = lam_dt
    rows = []
    for n in (float(CHUNK), float(seg_tokens)):
        mag = jnp.exp(lr_dt * n)
        for d in range(2):
            rows.append((mag[d] * jnp.cos(li_dt[d] * n)).reshape(SSM_GROUPS // 2, 2 * SSM_STATE))
            rows.append((mag[d] * jnp.sin(li_dt[d] * n)).reshape(SSM_GROUPS // 2, 2 * SSM_STATE))
    return jnp.stack(rows, axis=1)


def _attn_kernel(q_ref, ka_ref, kb_ref, vt_ref, o_ref, m_ref, acc_ref, st_ref, *, tk, nk):
    q2 = q_ref[0]
    k_refs = (ka_ref, kb_ref)
    m_ref[...] = jnp.full(m_ref.shape, -jnp.inf, F32)
    acc_ref[...] = jnp.zeros(acc_ref.shape, F32)
    ones_rows = jnp.ones((ONES_ROWS, tk), BF16)

    def scores(blk, buf):
        start = pl.multiple_of(blk * tk, tk)
        for hh in range(2):
            kk = k_refs[hh][0, pl.ds(start, tk), :]
            st_ref[buf, hh] = lax.dot_general(kk, q2, (((1,), (1,)), ((), ())),
                                              preferred_element_type=F32)

    def consume(blk, buf):
        start = pl.multiple_of(blk * tk, tk)
        vt1 = jnp.concatenate([vt_ref[0, :, pl.ds(start, tk)], ones_rows], axis=0)
        for hh in range(2):
            st = st_ref[buf, hh]
            m_old = m_ref[hh]
            m_new = jnp.maximum(m_old, jnp.max(st, axis=0, keepdims=True))
            alpha = jnp.exp2(m_old - m_new)
            p = jnp.exp2(st - m_new).astype(BF16)
            acc_ref[hh] = alpha * acc_ref[hh] + jnp.dot(vt1, p, preferred_element_type=F32)
            m_ref[hh] = m_new

    scores(0, 0)

    def body(i2, carry):
        i = 2 * i2
        scores(i + 1, 1)
        consume(i, 0)
        scores(jnp.minimum(i + 2, nk - 1), 0)
        consume(i + 1, 1)
        return carry

    lax.fori_loop(0, nk // 2, body, 0)
    o = jnp.concatenate([acc_ref[hh, :HEAD_DIM] / acc_ref[hh, HEAD_DIM:HEAD_DIM + 1] for hh in range(2)],
                        axis=0)
    o_ref[0] = jnp.transpose(o).astype(o_ref.dtype)


def _attention(q, kpad, vt, tq, tk):
    b, l, _ = q.shape
    nq = l // tq
    nk = l // tk
    assert nk % 2 == 0, "the kv loop is unrolled over two score buffers"
    kern = functools.partial(_attn_kernel, tk=tk, nk=nk)
    return pl.pallas_call(
        kern,
        grid=(b, ATTN_WIDTH // LANES, nq),
        in_specs=[
            pl.BlockSpec((1, tq, LANES), lambda i, j, n: (i, n, j)),
            pl.BlockSpec((1, l, LANES), lambda i, j, n: (i, 0, 2 * (j // 2))),
            pl.BlockSpec((1, l, LANES), lambda i, j, n: (i, 0, 2 * (j // 2) + 1)),
            pl.BlockSpec((1, HEAD_DIM, l), lambda i, j, n: (i, j // 2, 0)),
        ],
        out_specs=pl.BlockSpec((1, tq, LANES), lambda i, j, n: (i, n, j)),
        out_shape=jax.ShapeDtypeStruct((b, l, ATTN_WIDTH), BF16),
        scratch_shapes=[pltpu.VMEM((2, 1, tq), F32),
                        pltpu.VMEM((2, HEAD_DIM + ONES_ROWS, tq), F32),
                        pltpu.VMEM((2, 2, tk, tq), F32)],
        compiler_params=_cparams(("parallel", "parallel", "parallel")),
        name="attention",
    )(q, kpad, kpad, vt)


def _mix_out_kernel(x_ref, ys_ref, o_ref, wglu_ref, bglu_ref, gs_ref, go_ref, wos_ref, woo_ref, g2_ref,
                    x1_ref, h2_ref):
    s = jax.nn.gelu(ys_ref[0].astype(F32))
    gate = jnp.dot(s.astype(BF16), wglu_ref[...], preferred_element_type=F32) + bglu_ref[...]
    s = s * jax.nn.sigmoid(gate)
    sn = _rms(s, gs_ref[...]).astype(BF16)
    on = _rms(o_ref[0].astype(F32), go_ref[...]).astype(BF16)
    x1 = (x_ref[0]
          + jnp.dot(sn, wos_ref[...], preferred_element_type=F32)
          + jnp.dot(on, woo_ref[...], preferred_element_type=F32))
    x1_ref[0] = x1
    h2_ref[0] = _rms(x1, g2_ref[...]).astype(BF16)


def _mix_out(x, ys, o, w_glu, b_glu, gs, go, wo_s, wo_o, g2, tm):
    b, l, _ = x.shape
    tok = lambda w: pl.BlockSpec((1, tm, w), lambda i, j: (i, j, 0))
    return pl.pallas_call(
        _mix_out_kernel,
        grid=(b, l // tm),
        in_specs=[
            tok(D_MODEL), tok(SSM_WIDTH), tok(ATTN_WIDTH),
            _const_spec((SSM_WIDTH, SSM_WIDTH)), _const_spec((1, SSM_WIDTH)),
            _const_spec((1, SSM_WIDTH)), _const_spec((1, ATTN_WIDTH)),
            _const_spec((SSM_WIDTH, D_MODEL)), _const_spec((ATTN_WIDTH, D_MODEL)),
            _const_spec((1, D_MODEL)),
        ],
        out_specs=[tok(D_MODEL), tok(D_MODEL)],
        out_shape=[jax.ShapeDtypeStruct((b, l, D_MODEL), F32),
                   jax.ShapeDtypeStruct((b, l, D_MODEL), BF16)],
        compiler_params=_cparams(("parallel", "parallel")),
        name="mix_out",
    )(x, ys, o, w_glu, b_glu, gs, go, wo_s, wo_o, g2)


def _ffn_kernel(h2_ref, hp_ref, hn_ref, x1_ref, wg_ref, wu_ref, cw_ref, cb_ref, wd_ref, gf_ref,
                y_ref, acc_ref):
    i = pl.program_id(1)
    last = pl.num_programs(1) - 1
    h2 = h2_ref[0]
    tm = h2.shape[0]
    hcat = jnp.concatenate([hp_ref[0], h2, hn_ref[0]], axis=0)
    rows = tm + 2 * HALO
    row = lax.broadcasted_iota(jnp.int32, (tm, 1), 0)
    keep_prev = jnp.logical_not(jnp.logical_and(row == 0, i == 0))
    keep_next = jnp.logical_not(jnp.logical_and(row == tm - 1, i == last))
    acc_ref[...] = x1_ref[0]
    for c in range(D_FF // FF_CHUNK):
        cs = slice(c * FF_CHUNK, (c + 1) * FF_CHUNK)
        g = jnp.dot(hcat, wg_ref[:, cs], preferred_element_type=F32)
        g_prev = jnp.where(keep_prev, pltpu.roll(g, 1, 0)[HALO:HALO + tm], 0.0)
        g_next = jnp.where(keep_next, pltpu.roll(g, rows - 1, 0)[HALO:HALO + tm], 0.0)
        conv = (cb_ref[:, cs] + g_prev * cw_ref[0:1, cs] + g[HALO:HALO + tm] * cw_ref[1:2, cs]
                + g_next * cw_ref[2:3, cs])
        up = jnp.dot(h2, wu_ref[:, cs], preferred_element_type=F32)
        act = (jax.nn.silu(conv) * up).astype(BF16)
        acc_ref[...] += jnp.dot(act, wd_ref[cs, :], preferred_element_type=F32)
    y_ref[0] = _rms(acc_ref[...], gf_ref[...])


def _ffn(h2, x1, w_gate, w_up, conv_w, conv_b, w_down, gf, tm):
    b, l, _ = x1.shape
    nh = tm // HALO
    nblk = l // HALO
    tok = lambda: pl.BlockSpec((1, tm, D_MODEL), lambda i, j: (i, j, 0))
    return pl.pallas_call(
        _ffn_kernel,
        grid=(b, l // tm),
        in_specs=[
            tok(),
            pl.BlockSpec((1, HALO, D_MODEL), lambda i, j: (i, jnp.maximum(j * nh - 1, 0), 0)),
            pl.BlockSpec((1, HALO, D_MODEL), lambda i, j: (i, jnp.minimum((j + 1) * nh, nblk - 1), 0)),
            tok(),
            _const_spec((D_MODEL, D_FF)), _const_spec((D_MODEL, D_FF)),
            _const_spec((8, D_FF)), _const_spec((1, D_FF)),
            _const_spec((D_FF, D_MODEL)), _const_spec((1, D_MODEL)),
        ],
        out_specs=tok(),
        out_shape=jax.ShapeDtypeStruct((b, l, D_MODEL), F32),
        scratch_shapes=[pltpu.VMEM((tm, D_MODEL), F32)],
        compiler_params=_cparams(("parallel", "parallel")),
        name="ffn",
    )(h2, h2, h2, x1, w_gate, w_up, conv_w, conv_b, w_down, gf)


def _rope_tables(seq_len):
    rows = seq_len // GRID_W
    row_id = jnp.broadcast_to(jnp.arange(rows)[:, None], (rows, GRID_W)).reshape(-1).astype(F32)
    col_id = jnp.broadcast_to(jnp.arange(GRID_W)[None, :], (rows, GRID_W)).reshape(-1).astype(F32)
    axis_dim = HEAD_DIM // 2
    inv_freq = ROPE_THETA ** (-jnp.arange(0, axis_dim, 2, dtype=F32) / axis_dim)
    ang = jnp.concatenate([row_id[:, None] * inv_freq, col_id[:, None] * inv_freq], axis=-1)
    cos, sin = jnp.cos(ang), jnp.sin(ang)
    cos_t = jnp.tile(cos, (1, LANES // (HEAD_DIM // 2)))
    sin_t = jnp.tile(jnp.concatenate([-sin, sin], axis=-1), (1, LANES // HEAD_DIM))
    return cos_t, sin_t


def _encoder(x, p, tm, tq, tk, tm_ffn):
    b, l, _ = x.shape
    cos_t, sin_t = _rope_tables(l)
    u, q, kpad, vt = _in_proj(x, p["g1"], p["w_in"], p["ones"], p["qkg"], cos_t, sin_t, tm)

    chain = 1 if b % 8 == 0 else 8 // b
    nb = b * chain
    nc = l // (chain * CHUNK)
    a_mat = _ssm_decay_rows(p["lam_dt"], nc * CHUNK)
    ug = u.reshape(nb, nc, CHUNK, SSM_GROUPS, SSM_GROUP)
    ug = jnp.transpose(ug, (3, 1, 0, 2, 4)).reshape(SSM_GROUPS, nc * nb, CHUNK_W)
    yg = _ssm(ug, p["t_mat"], p["w_mat"], p["v_mat"], a_mat, nb, nc, chain)
    ys = jnp.transpose(yg.reshape(SSM_GROUPS, nc, nb, CHUNK, SSM_GROUP), (2, 1, 3, 0, 4)).reshape(b, l, SSM_WIDTH)

    o = _attention(q, kpad, vt, tq, tk)
    x1, h2 = _mix_out(x, ys, o, p["w_glu"], p["b_glu"], p["gs"], p["go"], p["wo_s"], p["wo_o"], p["g2"], tm)
    return _ffn(h2, x1, p["w_gate"], p["w_up"], p["conv_w"], p["conv_b"], p["w_down"], p["gf"], tm_ffn)


def _prepare(norm1_g, w_in, lam_re, lam_im, log_dt, b_re, b_im, c_re, c_im, d_skip, w_glu, b_glu,
             q_norm_g, k_norm_g, ssm_out_g, attn_out_g, w_out, norm2_g, w_gate, w_up, conv_w, conv_b,
             w_down, final_norm_g):
    i = 0
    t_mat, w_mat, v_mat, lam_dt = _ssm_matrices(lam_re[i], lam_im[i], log_dt[i], b_re[i], b_im[i],
                                                c_re[i], c_im[i], d_skip[i])
    head = jnp.arange(QK_WIDTH) // HEAD_DIM
    ones = (head[:, None] == head[None, :]).astype(BF16)
    qkg = jnp.concatenate([jnp.tile(q_norm_g[i], N_HEADS), jnp.tile(k_norm_g[i], N_KV_HEADS)])[None].astype(F32)
    row = lambda v: v[None].astype(F32)
    return dict(
        g1=row(norm1_g[i]), w_in=w_in[i].astype(BF16), ones=ones, qkg=qkg,
        t_mat=t_mat, w_mat=w_mat, v_mat=v_mat, lam_dt=lam_dt,
        w_glu=w_glu[i].astype(BF16), b_glu=row(b_glu[i]), gs=row(ssm_out_g[i]), go=row(attn_out_g[i]),
        wo_s=w_out[i, :SSM_WIDTH].astype(BF16), wo_o=w_out[i, SSM_WIDTH:].astype(BF16), g2=row(norm2_g[i]),
        w_gate=w_gate[i].astype(BF16), w_up=w_up[i].astype(BF16),
        conv_w=jnp.concatenate([conv_w[i].astype(F32), jnp.zeros((5, D_FF), F32)], axis=0),
        conv_b=row(conv_b[i]), w_down=w_down[i].astype(BF16), gf=row(final_norm_g),
    )


def kernel(x_prompt, x_sample, norm1_g, w_in, lam_re, lam_im, log_dt, b_re, b_im, c_re, c_im, d_skip, w_glu, b_glu, q_norm_g, k_norm_g, ssm_out_g, attn_out_g, w_out, norm2_g, w_gate, w_up, conv_w, conv_b, w_down, final_norm_g):
    p = _prepare(norm1_g, w_in, lam_re, lam_im, log_dt, b_re, b_im, c_re, c_im, d_skip, w_glu, b_glu,
                 q_norm_g, k_norm_g, ssm_out_g, attn_out_g, w_out, norm2_g, w_gate, w_up, conv_w, conv_b,
                 w_down, final_norm_g)
    y_prompt = _encoder(x_prompt, p, tm=512, tq=512, tk=512, tm_ffn=512)
    y_sample = _encoder(x_sample, p, tm=512, tq=512, tk=512, tm_ffn=512)
    return (y_prompt, y_sample)
```

```python
import functools
import math

import jax
import jax.numpy as jnp
from jax import lax
from jax.experimental import pallas as pl
from jax.experimental.pallas import tpu as pltpu

F32 = jnp.float32
BF16 = jnp.bfloat16

D_MODEL = 1024
SSM_WIDTH = 512
SSM_GROUP = 16
SSM_GROUPS = 32
SSM_STATE = 64
N_HEADS = 8
N_KV_HEADS = 2
HEAD_DIM = 64
ATTN_WIDTH = 512
KV_WIDTH = 128
IN_WIDTH = 1280
D_FF = 2816
GRID_W = 64
ROPE_THETA = 10000.0
EPS = 1e-6

LANES = 128
CHUNK = 16
CHUNK_W = CHUNK * SSM_GROUP
QK_WIDTH = ATTN_WIDTH + KV_WIDTH
SSM_ROW_BLOCK = 1024
FF_CHUNK = 256
HALO = 16
ONES_ROWS = 16
Q_SCALE = HEAD_DIM ** -0.5 * math.log2(math.e)
VMEM_LIMIT = 56 * 1024 * 1024


def _cparams(sem):
    return pltpu.CompilerParams(dimension_semantics=sem, vmem_limit_bytes=VMEM_LIMIT)


def _const_spec(shape):
    nd = len(shape)
    return pl.BlockSpec(shape, lambda *_: (0,) * nd, pipeline_mode=pl.Buffered(1))


def _rms(x, g):
    return x * lax.rsqrt(jnp.mean(x * x, axis=-1, keepdims=True) + EPS) * g


def _in_proj_kernel(x_ref, g1_ref, w_ref, ones_ref, qkg_ref, cos_ref, sin_ref,
                    u_ref, q_ref, k_ref, vt_ref):
    x = x_ref[0]
    h = _rms(x, g1_ref[...]).astype(BF16)
    z = jnp.dot(h, w_ref[...], preferred_element_type=F32)
    u_ref[0] = z[:, :SSM_WIDTH].astype(BF16)

    zqk = z[:, SSM_WIDTH:SSM_WIDTH + QK_WIDTH]
    sq = zqk * zqk
    sq_hi = sq.astype(BF16)
    sq_lo = (sq - sq_hi.astype(F32)).astype(BF16)
    ones = ones_ref[...]
    ss = (jnp.dot(sq_hi, ones, preferred_element_type=F32)
          + jnp.dot(sq_lo, ones, preferred_element_type=F32))
    y = zqk * lax.rsqrt(ss * (1.0 / HEAD_DIM) + EPS) * qkg_ref[...]

    tm = x.shape[0]
    lane = lax.broadcasted_iota(jnp.int32, (tm, LANES), 1)
    first_half = (lane % HEAD_DIM) < (HEAD_DIM // 2)
    c = cos_ref[...]
    s = sin_ref[...]
    for j in range(QK_WIDTH // LANES):
        yj = y[:, j * LANES:(j + 1) * LANES]
        partner = jnp.where(first_half,
                            pltpu.roll(yj, LANES - HEAD_DIM // 2, 1),
                            pltpu.roll(yj, HEAD_DIM // 2, 1))
        r = yj * c + partner * s
        if j < ATTN_WIDTH // LANES:
            q_ref[0, :, j * LANES:(j + 1) * LANES] = (r * Q_SCALE).astype(BF16)
        else:
            low = lane < HEAD_DIM
            k_lo = jnp.where(low, r, 0.0)
            k_hi = jnp.where(low, 0.0, r)
            k_ref[0, :, 0 * LANES:1 * LANES] = k_lo.astype(BF16)
            k_ref[0, :, 1 * LANES:2 * LANES] = pltpu.roll(k_lo, HEAD_DIM, 1).astype(BF16)
            k_ref[0, :, 2 * LANES:3 * LANES] = pltpu.roll(k_hi, HEAD_DIM, 1).astype(BF16)
            k_ref[0, :, 3 * LANES:4 * LANES] = k_hi.astype(BF16)

    v = z[:, SSM_WIDTH + QK_WIDTH:]
    vt_ref[0] = jnp.transpose(v).astype(BF16)


def _in_proj(x, g1, w_in, ones, qkg, cos_t, sin_t, tm):
    b, l, _ = x.shape
    nl = l // tm
    return pl.pallas_call(
        _in_proj_kernel,
        grid=(b, nl),
        in_specs=[
            pl.BlockSpec((1, tm, D_MODEL), lambda i, j: (i, j, 0)),
            _const_spec((1, D_MODEL)),
            _const_spec((D_MODEL, IN_WIDTH)),
            _const_spec((QK_WIDTH, QK_WIDTH)),
            _const_spec((1, QK_WIDTH)),
            pl.BlockSpec((tm, LANES), lambda i, j: (j, 0)),
            pl.BlockSpec((tm, LANES), lambda i, j: (j, 0)),
        ],
        out_specs=[
            pl.BlockSpec((1, tm, SSM_WIDTH), lambda i, j: (i, j, 0)),
            pl.BlockSpec((1, tm, ATTN_WIDTH), lambda i, j: (i, j, 0)),
            pl.BlockSpec((1, tm, 4 * LANES), lambda i, j: (i, j, 0)),
            pl.BlockSpec((1, KV_WIDTH, tm), lambda i, j: (i, 0, j)),
        ],
        out_shape=[
            jax.ShapeDtypeStruct((b, l, SSM_WIDTH), BF16),
            jax.ShapeDtypeStruct((b, l, ATTN_WIDTH), BF16),
            jax.ShapeDtypeStruct((b, l, 4 * LANES), BF16),
            jax.ShapeDtypeStruct((b, KV_WIDTH, l), BF16),
        ],
        compiler_params=_cparams(("parallel", "parallel")),
        name="in_proj",
    )(x, g1, w_in, ones, qkg, cos_t, sin_t)


def _ssm_kernel(u_ref, t_ref, w_ref, v_ref, a_ref, y_ref, z_ref, sp_ref, e_ref, s_ref, *, nb, nc, chain):
    r = z_ref.shape[0]
    rb = min(r, SSM_ROW_BLOCK)

    def row_block(j):
        return pl.ds(pl.multiple_of(j * rb, rb), rb)

    def inject(j, carry):
        rows = row_block(j)
        z_ref[rows, :] = (jnp.dot(u_ref[0, rows, :], w_ref[0, :CHUNK_W, :], preferred_element_type=F32)
                          + jnp.dot(u_ref[1, rows, :], w_ref[0, CHUNK_W:, :], preferred_element_type=F32))
        return carry

    lax.fori_loop(0, r // rb, inject, 0)

    a = a_ref[0]
    afr = jnp.broadcast_to(a[0:1], (nb, LANES))
    afi = jnp.broadcast_to(a[1:2], (nb, LANES))
    abr = jnp.broadcast_to(a[2:3], (nb, LANES))
    abi = jnp.broadcast_to(a[3:4], (nb, LANES))

    def advance(i, carry, store):
        fr, fi, br, bi = carry
        row_f = pl.multiple_of(i * nb, 8)
        row_b = pl.multiple_of((nc - 1 - i) * nb, 8)
        if store:
            sp_ref[pl.ds(row_f, nb), 0 * LANES:1 * LANES] = fr
            sp_ref[pl.ds(row_f, nb), 1 * LANES:2 * LANES] = fi
            sp_ref[pl.ds(row_b, nb), 2 * LANES:3 * LANES] = br
            sp_ref[pl.ds(row_b, nb), 3 * LANES:4 * LANES] = bi
        zfr = z_ref[pl.ds(row_f, nb), 0 * LANES:1 * LANES]
        zfi = z_ref[pl.ds(row_f, nb), 1 * LANES:2 * LANES]
        zbr = z_ref[pl.ds(row_b, nb), 2 * LANES:3 * LANES]
        zbi = z_ref[pl.ds(row_b, nb), 3 * LANES:4 * LANES]
        return (afr * fr - afi * fi + zfr, afr * fi + afi * fr + zfi,
                abr * br - abi * bi + zbr, abr * bi + abi * br + zbi)

    zero = jnp.zeros((nb, LANES), F32)
    init = (zero, zero, zero, zero)
    if chain > 1:
        ends = lax.fori_loop(0, nc, functools.partial(advance, store=False), init)
        for k in range(4):
            e_ref[k] = ends[k]
        for base in range(0, nb, chain):
            for comp, order in ((0, range(base, base + chain)), (2, range(base + chain - 1, base - 1, -1))):
                sr = a[4 + comp:5 + comp]
                si = a[5 + comp:6 + comp]
                cr = jnp.zeros((1, LANES), F32)
                ci = jnp.zeros((1, LANES), F32)
                for row in order:
                    s_ref[comp, row:row + 1, :] = cr
                    s_ref[comp + 1, row:row + 1, :] = ci
                    er = e_ref[comp, row:row + 1, :]
                    ei = e_ref[comp + 1, row:row + 1, :]
                    cr, ci = sr * cr - si * ci + er, sr * ci + si * cr + ei
        init = (s_ref[0], s_ref[1], s_ref[2], s_ref[3])
    lax.fori_loop(0, nc, functools.partial(advance, store=True), init)

    def readout(j, carry):
        rows = row_block(j)
        yv = jnp.dot(sp_ref[rows, :].astype(BF16), v_ref[0], preferred_element_type=F32)
        for k in range(2):
            yk = jnp.dot(u_ref[k, rows, :], t_ref[k], preferred_element_type=F32)
            y_ref[k, rows, :] = (yk + yv[:, k * CHUNK_W:(k + 1) * CHUNK_W]).astype(y_ref.dtype)
        return carry

    lax.fori_loop(0, r // rb, readout, 0)


def _ssm(ug, t_mat, w_mat, v_mat, a_mat, nb, nc, chain):
    g, r, _ = ug.shape
    kern = functools.partial(_ssm_kernel, nb=nb, nc=nc, chain=chain)
    return pl.pallas_call(
        kern,
        grid=(g // 2,),
        in_specs=[
            pl.BlockSpec((2, r, CHUNK_W), lambda i: (i, 0, 0)),
            pl.BlockSpec((2, CHUNK_W, CHUNK_W), lambda i: (i, 0, 0)),
            pl.BlockSpec((1, 2 * CHUNK_W, 2 * CHUNK_W), lambda i: (i, 0, 0)),
            pl.BlockSpec((1, 2 * CHUNK_W, 2 * CHUNK_W), lambda i: (i, 0, 0)),
            pl.BlockSpec((1, 8, LANES), lambda i: (i, 0, 0)),
        ],
        out_specs=pl.BlockSpec((2, r, CHUNK_W), lambda i: (i, 0, 0)),
        out_shape=jax.ShapeDtypeStruct((g, r, CHUNK_W), BF16),
        scratch_shapes=[pltpu.VMEM((r, 2 * CHUNK_W), F32), pltpu.VMEM((r, 2 * CHUNK_W), F32),
                        pltpu.VMEM((4, nb, LANES), F32), pltpu.VMEM((4, nb, LANES), F32)],
        compiler_params=_cparams(("parallel",)),
        name="ssm",
    )(ug, t_mat, w_mat, v_mat, a_mat)


def _ssm_matrices(lam_re, lam_im, log_dt, b_re, b_im, c_re, c_im, d_skip):
    g, p, hh = SSM_GROUPS, SSM_STATE, SSM_GROUP
    dt = jnp.exp(log_dt.astype(F32))[..., None]
    lr = lam_re.astype(F32)
    li = lam_im.astype(F32)
    mag = jnp.exp(lr * dt)
    ab_re = mag * jnp.cos(li * dt)
    ab_im = mag * jnp.sin(li * dt)
    den = lr * lr + li * li
    nr = ab_re - 1.0
    ni = ab_im
    zr = (nr * lr + ni * li) / den
    zi = (ni * lr - nr * li) / den
    br = b_re.astype(F32)
    bi = b_im.astype(F32)
    bb_re = zr[..., None] * br - zi[..., None] * bi
    bb_im = zr[..., None] * bi + zi[..., None] * br
    cr = c_re.astype(F32)
    ci = c_im.astype(F32)

    ks = jnp.arange(CHUNK + 1, dtype=F32)[:, None, None, None]
    pmag = jnp.exp(lr * dt * ks)
    pw_re = pmag * jnp.cos(li * dt * ks)
    pw_im = pmag * jnp.sin(li * dt * ks)

    cp_re = cr[None] * pw_re[:CHUNK, :, :, None, :] - ci[None] * pw_im[:CHUNK, :, :, None, :]
    cp_im = cr[None] * pw_im[:CHUNK, :, :, None, :] + ci[None] * pw_re[:CHUNK, :, :, None, :]
    kern = (jnp.einsum('kdgop,dgph->kdgoh', cp_re, bb_re, precision=lax.Precision.HIGHEST)
            - jnp.einsum('kdgop,dgph->kdgoh', cp_im, bb_im, precision=lax.Precision.HIGHEST))

    s_idx = jnp.arange(CHUNK)[:, None]
    t_idx = jnp.arange(CHUNK)[None, :]
    lag = t_idx - s_idx
    kf = kern[:, 0]
    kb = kern[:, 1]
    tf = jnp.where((lag >= 0)[..., None, None, None], kf[jnp.clip(lag, 0, CHUNK - 1)], 0.0)
    tb = jnp.where((lag <= 0)[..., None, None, None], kb[jnp.clip(-lag, 0, CHUNK - 1)], 0.0)
    tt = tf + tb
    eye_h = jnp.eye(hh, dtype=F32)
    skip = (lag == 0)[..., None, None, None] * (d_skip.astype(F32)[None, None, :, :, None] * eye_h)
    tt = tt + skip
    t_mat = jnp.transpose(tt, (2, 0, 4, 1, 3)).reshape(g, CHUNK_W, CHUNK_W)

    pf_re = pw_re[:CHUNK, 0][::-1]
    pf_im = pw_im[:CHUNK, 0][::-1]
    pb_re = pw_re[:CHUNK, 1]
    pb_im = pw_im[:CHUNK, 1]

    def inj(p_re, p_im, d):
        w_re = p_re[..., None] * bb_re[d][None] - p_im[..., None] * bb_im[d][None]
        w_im = p_re[..., None] * bb_im[d][None] + p_im[..., None] * bb_re[d][None]
        w_re = jnp.transpose(w_re, (1, 0, 3, 2)).reshape(g, CHUNK_W, p)
        w_im = jnp.transpose(w_im, (1, 0, 3, 2)).reshape(g, CHUNK_W, p)
        return w_re, w_im

    wf_re, wf_im = inj(pf_re, pf_im, 0)
    wb_re, wb_im = inj(pb_re, pb_im, 1)

    def pair_cols(m):
        m = m.reshape(g // 2, 2, CHUNK_W, p)
        zeros = jnp.zeros_like(m[:, 0])
        top = jnp.concatenate([m[:, 0], zeros], axis=-1)
        bot = jnp.concatenate([zeros, m[:, 1]], axis=-1)
        return jnp.concatenate([top, bot], axis=1)

    w_mat = jnp.concatenate([pair_cols(wf_re), pair_cols(wf_im), pair_cols(wb_re), pair_cols(wb_im)], axis=-1)

    def readout(q_re, q_im, d):
        v_re = cr[d][None] * q_re[:, :, None, :] - ci[d][None] * q_im[:, :, None, :]
        v_im = cr[d][None] * q_im[:, :, None, :] + ci[d][None] * q_re[:, :, None, :]
        v_re = jnp.transpose(v_re, (1, 3, 0, 2)).reshape(g, p, CHUNK_W)
        v_im = jnp.transpose(v_im, (1, 3, 0, 2)).reshape(g, p, CHUNK_W)
        return v_re, -v_im

    vf_re, vf_im = readout(pw_re[1:CHUNK + 1, 0], pw_im[1:CHUNK + 1, 0], 0)
    vb_re, vb_im = readout(pw_re[1:CHUNK + 1, 1][::-1], pw_im[1:CHUNK + 1, 1][::-1], 1)

    def pair_rows(m):
        m = m.reshape(g // 2, 2, p, CHUNK_W)
        zeros = jnp.zeros_like(m[:, 0])
        top = jnp.concatenate([m[:, 0], zeros], axis=-1)
        bot = jnp.concatenate([zeros, m[:, 1]], axis=-1)
        return jnp.concatenate([top, bot], axis=1)

    v_mat = jnp.concatenate([pair_rows(vf_re), pair_rows(vf_im), pair_rows(vb_re), pair_rows(vb_im)], axis=1)

    return t_mat.astype(BF16), w_mat.astype(BF16), v_mat.astype(BF16), (lr * dt, li * dt)


def _ssm_decay_rows(lam_dt, seg_tokens):
    lr_dt, li_dt = lam_dt
    rows = []
    for n in (float(CHUNK), float(seg_tokens)):
        mag = jnp.exp(lr_dt * n)
        for d in range(2):
            rows.append((mag[d] * jnp.cos(li_dt[d] * n)).reshape(SSM_GROUPS // 2, 2 * SSM_STATE))
            rows.append((mag[d] * jnp.sin(li_dt[d] * n)).reshape(SSM_GROUPS // 2, 2 * SSM_STATE))
    return jnp.stack(rows, axis=1)


def _attn_kernel(q_ref, ka_ref, kb_ref, vt_ref, o_ref, m_ref, acc_ref, st_ref, *, tk, nk):
    q2 = q_ref[0]
    k_refs = (ka_ref, kb_ref)
    m_ref[...] = jnp.full(m_ref.shape, -jnp.inf, F32)
    acc_ref[...] = jnp.zeros(acc_ref.shape, F32)
    ones_rows = jnp.ones((ONES_ROWS, tk), BF16)

    def scores(blk, buf):
        start = pl.multiple_of(blk * tk, tk)
        for hh in range(2):
            kk = k_refs[hh][0, pl.ds(start, tk), :]
            st_ref[buf, hh] = lax.dot_general(kk, q2, (((1,), (1,)), ((), ())),
                                              preferred_element_type=F32)

    def consume(blk, buf):
        start = pl.multiple_of(blk * tk, tk)
        vt1 = jnp.concatenate([vt_ref[0, :, pl.ds(start, tk)], ones_rows], axis=0)
        for hh in range(2):
            st = st_ref[buf, hh]
            m_old = m_ref[hh]
            m_new = jnp.maximum(m_old, jnp.max(st, axis=0, keepdims=True))
            alpha = jnp.exp2(m_old - m_new)
            p = jnp.exp2(st - m_new).astype(BF16)
            acc_ref[hh] = alpha * acc_ref[hh] + jnp.dot(vt1, p, preferred_element_type=F32)
            m_ref[hh] = m_new

    scores(0, 0)

    def body(i2, carry):
        i = 2 * i2
        scores(i + 1, 1)
        consume(i, 0)
        scores(jnp.minimum(i + 2, nk - 1), 0)
        consume(i + 1, 1)
        return carry

    lax.fori_loop(0, nk // 2, body, 0, unroll=2)
    o = jnp.concatenate([acc_ref[hh, :HEAD_DIM] / acc_ref[hh, HEAD_DIM:HEAD_DIM + 1] for hh in range(2)],
                        axis=0)
    o_ref[0] = jnp.transpose(o).astype(o_ref.dtype)


def _attention(q, kpad, vt, tq, tk):
    b, l, _ = q.shape
    nq = l // tq
    nk = l // tk
    assert nk % 4 == 0, "the kv loop handles two score buffers per trip, two trips per unrolled body"
    kern = functools.partial(_attn_kernel, tk=tk, nk=nk)
    return pl.pallas_call(
        kern,
        grid=(b, ATTN_WIDTH // LANES, nq),
        in_specs=[
            pl.BlockSpec((1, tq, LANES), lambda i, j, n: (i, n, j)),
            pl.BlockSpec((1, l, LANES), lambda i, j, n: (i, 0, 2 * (j // 2))),
            pl.BlockSpec((1, l, LANES), lambda i, j, n: (i, 0, 2 * (j // 2) + 1)),
            pl.BlockSpec((1, HEAD_DIM, l), lambda i, j, n: (i, j // 2, 0)),
        ],
        out_specs=pl.BlockSpec((1, tq, LANES), lambda i, j, n: (i, n, j)),
        out_shape=jax.ShapeDtypeStruct((b, l, ATTN_WIDTH), BF16),
        scratch_shapes=[pltpu.VMEM((2, 1, tq), F32),
                        pltpu.VMEM((2, HEAD_DIM + ONES_ROWS, tq), F32),
                        pltpu.VMEM((2, 2, tk, tq), F32)],
        compiler_params=_cparams(("parallel", "parallel", "parallel")),
        name="attention",
    )(q, kpad, kpad, vt)


def _mix_out_kernel(x_ref, ys_ref, o_ref, wglu_ref, bglu_ref, gs_ref, go_ref, wos_ref, woo_ref, g2_ref,
                    x1_ref, h2_ref):
    s = jax.nn.gelu(ys_ref[0].astype(F32))
    gate = jnp.dot(s.astype(BF16), wglu_ref[...], preferred_element_type=F32) + bglu_ref[...]
    s = s * jax.nn.sigmoid(gate)
    sn = _rms(s, gs_ref[...]).astype(BF16)
    on = _rms(o_ref[0].astype(F32), go_ref[...]).astype(BF16)
    x1 = (x_ref[0]
          + jnp.dot(sn, wos_ref[...], preferred_element_type=F32)
          + jnp.dot(on, woo_ref[...], preferred_element_type=F32))
    x1_ref[0] = x1
    h2_ref[0] = _rms(x1, g2_ref[...]).astype(BF16)


def _mix_out(x, ys, o, w_glu, b_glu, gs, go, wo_s, wo_o, g2, tm):
    b, l, _ = x.shape
    tok = lambda w: pl.BlockSpec((1, tm, w), lambda i, j: (i, j, 0))
    return pl.pallas_call(
        _mix_out_kernel,
        grid=(b, l // tm),
        in_specs=[
            tok(D_MODEL), tok(SSM_WIDTH), tok(ATTN_WIDTH),
            _const_spec((SSM_WIDTH, SSM_WIDTH)), _const_spec((1, SSM_WIDTH)),
            _const_spec((1, SSM_WIDTH)), _const_spec((1, ATTN_WIDTH)),
            _const_spec((SSM_WIDTH, D_MODEL)), _const_spec((ATTN_WIDTH, D_MODEL)),
            _const_spec((1, D_MODEL)),
        ],
        out_specs=[tok(D_MODEL), tok(D_MODEL)],
        out_shape=[jax.ShapeDtypeStruct((b, l, D_MODEL), F32),
                   jax.ShapeDtypeStruct((b, l, D_MODEL), BF16)],
        compiler_params=_cparams(("parallel", "parallel")),
        name="mix_out",
    )(x, ys, o, w_glu, b_glu, gs, go, wo_s, wo_o, g2)


def _ffn_kernel(h2_ref, hp_ref, hn_ref, x1_ref, wg_ref, wu_ref, cw_ref, cb_ref, wd_ref, gf_ref,
                y_ref, acc_ref):
    i = pl.program_id(1)
    last = pl.num_programs(1) - 1
    h2 = h2_ref[0]
    tm = h2.shape[0]
    hcat = jnp.concatenate([hp_ref[0], h2, hn_ref[0]], axis=0)
    rows = tm + 2 * HALO
    row = lax.broadcasted_iota(jnp.int32, (tm, 1), 0)
    keep_prev = jnp.logical_not(jnp.logical_and(row == 0, i == 0))
    keep_next = jnp.logical_not(jnp.logical_and(row == tm - 1, i == last))
    acc_ref[...] = x1_ref[0]
    for c in range(D_FF // FF_CHUNK):
        cs = slice(c * FF_CHUNK, (c + 1) * FF_CHUNK)
        g = jnp.dot(hcat, wg_ref[:, cs], preferred_element_type=F32)
        g_prev = jnp.where(keep_prev, pltpu.roll(g, 1, 0)[HALO:HALO + tm], 0.0)
        g_next = jnp.where(keep_next, pltpu.roll(g, rows - 1, 0)[HALO:HALO + tm], 0.0)
        conv = (cb_ref[:, cs] + g_prev * cw_ref[0:1, cs] + g[HALO:HALO + tm] * cw_ref[1:2, cs]
                + g_next * cw_ref[2:3, cs])
        up = jnp.dot(h2, wu_ref[:, cs], preferred_element_type=F32)
        act = (jax.nn.silu(conv) * up).astype(BF16)
        acc_ref[...] += jnp.dot(act, wd_ref[cs, :], preferred_element_type=F32)
    y_ref[0] = _rms(acc_ref[...], gf_ref[...])


def _ffn(h2, x1, w_gate, w_up, conv_w, conv_b, w_down, gf, tm):
    b, l, _ = x1.shape
    nh = tm // HALO
    nblk = l // HALO
    tok = lambda: pl.BlockSpec((1, tm, D_MODEL), lambda i, j: (i, j, 0))
    return pl.pallas_call(
        _ffn_kernel,
        grid=(b, l // tm),
        in_specs=[
            tok(),
            pl.BlockSpec((1, HALO, D_MODEL), lambda i, j: (i, jnp.maximum(j * nh - 1, 0), 0)),
            pl.BlockSpec((1, HALO, D_MODEL), lambda i, j: (i, jnp.minimum((j + 1) * nh, nblk - 1), 0)),
            tok(),
            _const_spec((D_MODEL, D_FF)), _const_spec((D_MODEL, D_FF)),
            _const_spec((8, D_FF)), _const_spec((1, D_FF)),
            _const_spec((D_FF, D_MODEL)), _const_spec((1, D_MODEL)),
        ],
        out_specs=tok(),
        out_shape=jax.ShapeDtypeStruct((b, l, D_MODEL), F32),
        scratch_shapes=[pltpu.VMEM((tm, D_MODEL), F32)],
        compiler_params=_cparams(("parallel", "parallel")),
        name="ffn",
    )(h2, h2, h2, x1, w_gate, w_up, conv_w, conv_b, w_down, gf)


def _rope_tables(seq_len):
    rows = seq_len // GRID_W
    row_id = jnp.broadcast_to(jnp.arange(rows)[:, None], (rows, GRID_W)).reshape(-1).astype(F32)
    col_id = jnp.broadcast_to(jnp.arange(GRID_W)[None, :], (rows, GRID_W)).reshape(-1).astype(F32)
    axis_dim = HEAD_DIM // 2
    inv_freq = ROPE_THETA ** (-jnp.arange(0, axis_dim, 2, dtype=F32) / axis_dim)
    ang = jnp.concatenate([row_id[:, None] * inv_freq, col_id[:, None] * inv_freq], axis=-1)
    cos, sin = jnp.cos(ang), jnp.sin(ang)
    cos_t = jnp.tile(cos, (1, LANES // (HEAD_DIM // 2)))
    sin_t = jnp.tile(jnp.concatenate([-sin, sin], axis=-1), (1, LANES // HEAD_DIM))
    return cos_t, sin_t


def _encoder(x, p, tm, tq, tk, tm_ffn):
    b, l, _ = x.shape
    cos_t, sin_t = _rope_tables(l)
    u, q, kpad, vt = _in_proj(x, p["g1"], p["w_in"], p["ones"], p["qkg"], cos_t, sin_t, tm)

    chain = 1 if b % 8 == 0 else 8 // b
    nb = b * chain
    nc = l // (chain * CHUNK)
    a_mat = _ssm_decay_rows(p["lam_dt"], nc * CHUNK)
    ug = u.reshape(nb, nc, CHUNK, SSM_GROUPS, SSM_GROUP)
    ug = jnp.transpose(ug, (3, 1, 0, 2, 4)).reshape(SSM_GROUPS, nc * nb, CHUNK_W)
    yg = _ssm(ug, p["t_mat"], p["w_mat"], p["v_mat"], a_mat, nb, nc, chain)
    ys = jnp.transpose(yg.reshape(SSM_GROUPS, nc, nb, CHUNK, SSM_GROUP), (2, 1, 3, 0, 4)).reshape(b, l, SSM_WIDTH)

    o = _attention(q, kpad, vt, tq, tk)
    x1, h2 = _mix_out(x, ys, o, p["w_glu"], p["b_glu"], p["gs"], p["go"], p["wo_s"], p["wo_o"], p["g2"], tm)
    return _ffn(h2, x1, p["w_gate"], p["w_up"], p["conv_w"], p["conv_b"], p["w_down"], p["gf"], tm_ffn)


def _prepare(norm1_g, w_in, lam_re, lam_im, log_dt, b_re, b_im, c_re, c_im, d_skip, w_glu, b_glu,
             q_norm_g, k_norm_g, ssm_out_g, attn_out_g, w_out, norm2_g, w_gate, w_up, conv_w, conv_b,
             w_down, final_norm_g):
    i = 0
    t_mat, w_mat, v_mat, lam_dt = _ssm_matrices(lam_re[i], lam_im[i], log_dt[i], b_re[i], b_im[i],
                                                c_re[i], c_im[i], d_skip[i])
    head = jnp.arange(QK_WIDTH) // HEAD_DIM
    ones = (head[:, None] == head[None, :]).astype(BF16)
    qkg = jnp.concatenate([jnp.tile(q_norm_g[i], N_HEADS), jnp.tile(k_norm_g[i], N_KV_HEADS)])[None].astype(F32)
    row = lambda v: v[None].astype(F32)
    return dict(
        g1=row(norm1_g[i]), w_in=w_in[i].astype(BF16), ones=ones, qkg=qkg,
        t_mat=t_mat, w_mat=w_mat, v_mat=v_mat, lam_dt=lam_dt,
        w_glu=w_glu[i].astype(BF16), b_glu=row(b_glu[i]), gs=row(ssm_out_g[i]), go=row(attn_out_g[i]),
        wo_s=w_out[i, :SSM_WIDTH].astype(BF16), wo_o=w_out[i, SSM_WIDTH:].astype(BF16), g2=row(norm2_g[i]),
        w_gate=w_gate[i].astype(BF16), w_up=w_up[i].astype(BF16),
        conv_w=jnp.concatenate([conv_w[i].astype(F32), jnp.zeros((5, D_FF), F32)], axis=0),
        conv_b=row(conv_b[i]), w_down=w_down[i].astype(BF16), gf=row(final_norm_g),
    )


def kernel(x_prompt, x_sample, norm1_g, w_in, lam_re, lam_im, log_dt, b_re, b_im, c_re, c_im, d_skip, w_glu, b_glu, q_norm_g, k_norm_g, ssm_out_g, attn_out_g, w_out, norm2_g, w_gate, w_up, conv_w, conv_b, w_down, final_norm_g):
    p = _prepare(norm1_g, w_in, lam_re, lam_im, log_dt, b_re, b_im, c_re, c_im, d_skip, w_glu, b_glu,
                 q_norm_g, k_norm_g, ssm_out_g, attn_out_g, w_out, norm2_g, w_gate, w_up, conv_w, conv_b,
                 w_down, final_norm_g)
    y_prompt = _encoder(x_prompt, p, tm=512, tq=512, tk=512, tm_ffn=512)
    y_sample = _encoder(x_sample, p, tm=512, tq=512, tk=512, tm_ffn=512)
    return (y_prompt, y_sample)
```

```python
import functools
import math

import jax
import jax.numpy as jnp
from jax import lax
from jax.experimental import pallas as pl
from jax.experimental.pallas import tpu as pltpu

F32 = jnp.float32
BF16 = jnp.bfloat16

D_MODEL = 1024
SSM_WIDTH = 512
SSM_GROUP = 16
SSM_GROUPS = 32
SSM_STATE = 64
N_HEADS = 8
N_KV_HEADS = 2
HEAD_DIM = 64
ATTN_WIDTH = 512
KV_WIDTH = 128
IN_WIDTH = 1280
D_FF = 2816
GRID_W = 64
ROPE_THETA = 10000.0
EPS = 1e-6

LANES = 128
CHUNK = 16
CHUNK_W = CHUNK * SSM_GROUP
QK_WIDTH = ATTN_WIDTH + KV_WIDTH
SSM_ROW_BLOCK = 1024
FF_CHUNK = 256
HALO = 16
ONES_ROWS = 16
Q_SCALE = HEAD_DIM ** -0.5 * math.log2(math.e)
VMEM_LIMIT = 56 * 1024 * 1024


def _cparams(sem):
    return pltpu.CompilerParams(dimension_semantics=sem, vmem_limit_bytes=VMEM_LIMIT)


def _const_spec(shape):
    nd = len(shape)
    return pl.BlockSpec(shape, lambda *_: (0,) * nd, pipeline_mode=pl.Buffered(1))


def _rms(x, g):
    return x * lax.rsqrt(jnp.mean(x * x, axis=-1, keepdims=True) + EPS) * g


def _in_proj_kernel(x_ref, g1_ref, w_ref, ones_ref, qkg_ref, cos_ref, sin_ref,
                    u_ref, q_ref, k_ref, vt_ref):
    x = x_ref[0]
    h = _rms(x, g1_ref[...]).astype(BF16)
    z = jnp.dot(h, w_ref[...], preferred_element_type=F32)
    u_ref[0] = z[:, :SSM_WIDTH].astype(BF16)

    zqk = z[:, SSM_WIDTH:SSM_WIDTH + QK_WIDTH]
    sq = zqk * zqk
    sq_hi = sq.astype(BF16)
    sq_lo = (sq - sq_hi.astype(F32)).astype(BF16)
    ones = ones_ref[...]
    ss = (jnp.dot(sq_hi, ones, preferred_element_type=F32)
          + jnp.dot(sq_lo, ones, preferred_element_type=F32))
    y = zqk * lax.rsqrt(ss * (1.0 / HEAD_DIM) + EPS) * qkg_ref[...]

    tm = x.shape[0]
    lane = lax.broadcasted_iota(jnp.int32, (tm, LANES), 1)
    first_half = (lane % HEAD_DIM) < (HEAD_DIM // 2)
    c = cos_ref[...]
    s = sin_ref[...]
    for j in range(QK_WIDTH // LANES):
        yj = y[:, j * LANES:(j + 1) * LANES]
        partner = jnp.where(first_half,
                            pltpu.roll(yj, LANES - HEAD_DIM // 2, 1),
                            pltpu.roll(yj, HEAD_DIM // 2, 1))
        r = yj * c + partner * s
        if j < ATTN_WIDTH // LANES:
            q_ref[0, :, j * LANES:(j + 1) * LANES] = (r * Q_SCALE).astype(BF16)
        else:
            low = lane < HEAD_DIM
            k_lo = jnp.where(low, r, 0.0)
            k_hi = jnp.where(low, 0.0, r)
            k_ref[0, :, 0 * LANES:1 * LANES] = k_lo.astype(BF16)
            k_ref[0, :, 1 * LANES:2 * LANES] = pltpu.roll(k_lo, HEAD_DIM, 1).astype(BF16)
            k_ref[0, :, 2 * LANES:3 * LANES] = pltpu.roll(k_hi, HEAD_DIM, 1).astype(BF16)
            k_ref[0, :, 3 * LANES:4 * LANES] = k_hi.astype(BF16)

    v = z[:, SSM_WIDTH + QK_WIDTH:]
    vt_ref[0] = jnp.transpose(v).astype(BF16)


def _in_proj(x, g1, w_in, ones, qkg, cos_t, sin_t, tm):
    b, l, _ = x.shape
    nl = l // tm
    return pl.pallas_call(
        _in_proj_kernel,
        grid=(b, nl),
        in_specs=[
            pl.BlockSpec((1, tm, D_MODEL), lambda i, j: (i, j, 0)),
            _const_spec((1, D_MODEL)),
            _const_spec((D_MODEL, IN_WIDTH)),
            _const_spec((QK_WIDTH, QK_WIDTH)),
            _const_spec((1, QK_WIDTH)),
            pl.BlockSpec((tm, LANES), lambda i, j: (j, 0)),
            pl.BlockSpec((tm, LANES), lambda i, j: (j, 0)),
        ],
        out_specs=[
            pl.BlockSpec((1, tm, SSM_WIDTH), lambda i, j: (i, j, 0)),
            pl.BlockSpec((1, tm, ATTN_WIDTH), lambda i, j: (i, j, 0)),
            pl.BlockSpec((1, tm, 4 * LANES), lambda i, j: (i, j, 0)),
            pl.BlockSpec((1, KV_WIDTH, tm), lambda i, j: (i, 0, j)),
        ],
        out_shape=[
            jax.ShapeDtypeStruct((b, l, SSM_WIDTH), BF16),
            jax.ShapeDtypeStruct((b, l, ATTN_WIDTH), BF16),
            jax.ShapeDtypeStruct((b, l, 4 * LANES), BF16),
            jax.ShapeDtypeStruct((b, KV_WIDTH, l), BF16),
        ],
        compiler_params=_cparams(("parallel", "parallel")),
        name="in_proj",
    )(x, g1, w_in, ones, qkg, cos_t, sin_t)


def _ssm_kernel(u_ref, t_ref, w_ref, v_ref, a_ref, y_ref, z_ref, sp_ref, e_ref, s_ref, *, nb, nc, chain):
    r = z_ref.shape[0]
    rb = min(r, SSM_ROW_BLOCK)

    def row_block(j):
        return pl.ds(pl.multiple_of(j * rb, rb), rb)

    def inject(j, carry):
        rows = row_block(j)
        z_ref[rows, :] = (jnp.dot(u_ref[0, rows, :], w_ref[0, :CHUNK_W, :], preferred_element_type=F32)
                          + jnp.dot(u_ref[1, rows, :], w_ref[0, CHUNK_W:, :], preferred_element_type=F32))
        return carry

    lax.fori_loop(0, r // rb, inject, 0)

    a = a_ref[0]
    afr = jnp.broadcast_to(a[0:1], (nb, LANES))
    afi = jnp.broadcast_to(a[1:2], (nb, LANES))
    abr = jnp.broadcast_to(a[2:3], (nb, LANES))
    abi = jnp.broadcast_to(a[3:4], (nb, LANES))

    def advance(i, carry, store):
        fr, fi, br, bi = carry
        row_f = pl.multiple_of(i * nb, 8)
        row_b = pl.multiple_of((nc - 1 - i) * nb, 8)
        if store:
            sp_ref[pl.ds(row_f, nb), 0 * LANES:1 * LANES] = fr
            sp_ref[pl.ds(row_f, nb), 1 * LANES:2 * LANES] = fi
            sp_ref[pl.ds(row_b, nb), 2 * LANES:3 * LANES] = br
            sp_ref[pl.ds(row_b, nb), 3 * LANES:4 * LANES] = bi
        zfr = z_ref[pl.ds(row_f, nb), 0 * LANES:1 * LANES]
        zfi = z_ref[pl.ds(row_f, nb), 1 * LANES:2 * LANES]
        zbr = z_ref[pl.ds(row_b, nb), 2 * LANES:3 * LANES]
        zbi = z_ref[pl.ds(row_b, nb), 3 * LANES:4 * LANES]
        return (afr * fr - afi * fi + zfr, afr * fi + afi * fr + zfi,
                abr * br - abi * bi + zbr, abr * bi + abi * br + zbi)

    zero = jnp.zeros((nb, LANES), F32)
    init = (zero, zero, zero, zero)
    if chain > 1:
        ends = lax.fori_loop(0, nc, functools.partial(advance, store=False), init)
        for k in range(4):
            e_ref[k] = ends[k]
        for base in range(0, nb, chain):
            for comp, order in ((0, range(base, base + chain)), (2, range(base + chain - 1, base - 1, -1))):
                sr = a[4 + comp:5 + comp]
                si = a[5 + comp:6 + comp]
                cr = jnp.zeros((1, LANES), F32)
                ci = jnp.zeros((1, LANES), F32)
                for row in order:
                    s_ref[comp, row:row + 1, :] = cr
                    s_ref[comp + 1, row:row + 1, :] = ci
                    er = e_ref[comp, row:row + 1, :]
                    ei = e_ref[comp + 1, row:row + 1, :]
                    cr, ci = sr * cr - si * ci + er, sr * ci + si * cr + ei
        init = (s_ref[0], s_ref[1], s_ref[2], s_ref[3])
    lax.fori_loop(0, nc, functools.partial(advance, store=True), init)

    def readout(j, carry):
        rows = row_block(j)
        yv = jnp.dot(sp_ref[rows, :].astype(BF16), v_ref[0], preferred_element_type=F32)
        for k in range(2):
            yk = jnp.dot(u_ref[k, rows, :], t_ref[k], preferred_element_type=F32)
            y_ref[k, rows, :] = (yk + yv[:, k * CHUNK_W:(k + 1) * CHUNK_W]).astype(y_ref.dtype)
        return carry

    lax.fori_loop(0, r // rb, readout, 0)


def _ssm(ug, t_mat, w_mat, v_mat, a_mat, nb, nc, chain):
    g, r, _ = ug.shape
    kern = functools.partial(_ssm_kernel, nb=nb, nc=nc, chain=chain)
    return pl.pallas_call(
        kern,
        grid=(g // 2,),
        in_specs=[
            pl.BlockSpec((2, r, CHUNK_W), lambda i: (i, 0, 0)),
            pl.BlockSpec((2, CHUNK_W, CHUNK_W), lambda i: (i, 0, 0)),
            pl.BlockSpec((1, 2 * CHUNK_W, 2 * CHUNK_W), lambda i: (i, 0, 0)),
            pl.BlockSpec((1, 2 * CHUNK_W, 2 * CHUNK_W), lambda i: (i, 0, 0)),
            pl.BlockSpec((1, 8, LANES), lambda i: (i, 0, 0)),
        ],
        out_specs=pl.BlockSpec((2, r, CHUNK_W), lambda i: (i, 0, 0)),
        out_shape=jax.ShapeDtypeStruct((g, r, CHUNK_W), BF16),
        scratch_shapes=[pltpu.VMEM((r, 2 * CHUNK_W), F32), pltpu.VMEM((r, 2 * CHUNK_W), F32),
                        pltpu.VMEM((4, nb, LANES), F32), pltpu.VMEM((4, nb, LANES), F32)],
        compiler_params=_cparams(("parallel",)),
        name="ssm",
    )(ug, t_mat, w_mat, v_mat, a_mat)


def _ssm_matrices(lam_re, lam_im, log_dt, b_re, b_im, c_re, c_im, d_skip):
    g, p, hh = SSM_GROUPS, SSM_STATE, SSM_GROUP
    dt = jnp.exp(log_dt.astype(F32))[..., None]
    lr = lam_re.astype(F32)
    li = lam_im.astype(F32)
    mag = jnp.exp(lr * dt)
    ab_re = mag * jnp.cos(li * dt)
    ab_im = mag * jnp.sin(li * dt)
    den = lr * lr + li * li
    nr = ab_re - 1.0
    ni = ab_im
    zr = (nr * lr + ni * li) / den
    zi = (ni * lr - nr * li) / den
    br = b_re.astype(F32)
    bi = b_im.astype(F32)
    bb_re = zr[..., None] * br - zi[..., None] * bi
    bb_im = zr[..., None] * bi + zi[..., None] * br
    cr = c_re.astype(F32)
    ci = c_im.astype(F32)

    ks = jnp.arange(CHUNK + 1, dtype=F32)[:, None, None, None]
    pmag = jnp.exp(lr * dt * ks)
    pw_re = pmag * jnp.cos(li * dt * ks)
    pw_im = pmag * jnp.sin(li * dt * ks)

    cp_re = cr[None] * pw_re[:CHUNK, :, :, None, :] - ci[None] * pw_im[:CHUNK, :, :, None, :]
    cp_im = cr[None] * pw_im[:CHUNK, :, :, None, :] + ci[None] * pw_re[:CHUNK, :, :, None, :]
    kern = (jnp.einsum('kdgop,dgph->kdgoh', cp_re, bb_re, precision=lax.Precision.HIGHEST)
            - jnp.einsum('kdgop,dgph->kdgoh', cp_im, bb_im, precision=lax.Precision.HIGHEST))

    s_idx = jnp.arange(CHUNK)[:, None]
    t_idx = jnp.arange(CHUNK)[None, :]
    lag = t_idx - s_idx
    kf = kern[:, 0]
    kb = kern[:, 1]
    tf = jnp.where((lag >= 0)[..., None, None, None], kf[jnp.clip(lag, 0, CHUNK - 1)], 0.0)
    tb = jnp.where((lag <= 0)[..., None, None, None], kb[jnp.clip(-lag, 0, CHUNK - 1)], 0.0)
    tt = tf + tb
    eye_h = jnp.eye(hh, dtype=F32)
    skip = (lag == 0)[..., None, None, None] * (d_skip.astype(F32)[None, None, :, :, None] * eye_h)
    tt = tt + skip
    t_mat = jnp.transpose(tt, (2, 0, 4, 1, 3)).reshape(g, CHUNK_W, CHUNK_W)

    pf_re = pw_re[:CHUNK, 0][::-1]
    pf_im = pw_im[:CHUNK, 0][::-1]
    pb_re = pw_re[:CHUNK, 1]
    pb_im = pw_im[:CHUNK, 1]

    def inj(p_re, p_im, d):
        w_re = p_re[..., None] * bb_re[d][None] - p_im[..., None] * bb_im[d][None]
        w_im = p_re[..., None] * bb_im[d][None] + p_im[..., None] * bb_re[d][None]
        w_re = jnp.transpose(w_re, (1, 0, 3, 2)).reshape(g, CHUNK_W, p)
        w_im = jnp.transpose(w_im, (1, 0, 3, 2)).reshape(g, CHUNK_W, p)
        return w_re, w_im

    wf_re, wf_im = inj(pf_re, pf_im, 0)
    wb_re, wb_im = inj(pb_re, pb_im, 1)

    def pair_cols(m):
        m = m.reshape(g // 2, 2, CHUNK_W, p)
        zeros = jnp.zeros_like(m[:, 0])
        top = jnp.concatenate([m[:, 0], zeros], axis=-1)
        bot = jnp.concatenate([zeros, m[:, 1]], axis=-1)
        return jnp.concatenate([top, bot], axis=1)

    w_mat = jnp.concatenate([pair_cols(wf_re), pair_cols(wf_im), pair_cols(wb_re), pair_cols(wb_im)], axis=-1)

    def readout(q_re, q_im, d):
        v_re = cr[d][None] * q_re[:, :, None, :] - ci[d][None] * q_im[:, :, None, :]
        v_im = cr[d][None] * q_im[:, :, None, :] + ci[d][None] * q_re[:, :, None, :]
        v_re = jnp.transpose(v_re, (1, 3, 0, 2)).reshape(g, p, CHUNK_W)
        v_im = jnp.transpose(v_im, (1, 3, 0, 2)).reshape(g, p, CHUNK_W)
        return v_re, -v_im

    vf_re, vf_im = readout(pw_re[1:CHUNK + 1, 0], pw_im[1:CHUNK + 1, 0], 0)
    vb_re, vb_im = readout(pw_re[1:CHUNK + 1, 1][::-1], pw_im[1:CHUNK + 1, 1][::-1], 1)

    def pair_rows(m):
        m = m.reshape(g // 2, 2, p, CHUNK_W)
        zeros = jnp.zeros_like(m[:, 0])
        top = jnp.concatenate([m[:, 0], zeros], axis=-1)
        bot = jnp.concatenate([zeros, m[:, 1]], axis=-1)
        return jnp.concatenate([top, bot], axis=1)

    v_mat = jnp.concatenate([pair_rows(vf_re), pair_rows(vf_im), pair_rows(vb_re), pair_rows(vb_im)], axis=1)

    return t_mat.astype(BF16), w_mat.astype(BF16), v_mat.astype(BF16), (lr * dt, li * dt)


def _ssm_decay_rows(lam_dt, seg_tokens):
    lr_dt, li_dt = lam_dt
    rows = []
    for n in (float(CHUNK), float(seg_tokens)):
        mag = jnp.exp(lr_dt * n)
        for d in range(2):
            rows.append((mag[d] * jnp.cos(li_dt[d] * n)).reshape(SSM_GROUPS // 2, 2 * SSM_STATE))
            rows.append((mag[d] * jnp.sin(li_dt[d] * n)).reshape(SSM_GROUPS // 2, 2 * SSM_STATE))
    return jnp.stack(rows, axis=1)


def _attn_kernel(q_ref, ka_ref, kb_ref, vt_ref, o_ref, m_ref, acc_ref, st_ref, *, tk, nk):
    q2 = q_ref[0]
    k_refs = (ka_ref, kb_ref)
    m_ref[...] = jnp.full(m_ref.shape, -jnp.inf, F32)
    acc_ref[...] = jnp.zeros(acc_ref.shape, F32)
    ones_rows = jnp.ones((ONES_ROWS, tk), BF16)

    def scores(blk, buf):
        start = pl.multiple_of(blk * tk, tk)
        for hh in range(2):
            kk = k_refs[hh][0, pl.ds(start, tk), :]
            st_ref[buf, hh] = lax.dot_general(kk, q2, (((1,), (1,)), ((), ())),
                                              preferred_element_type=F32)

    def consume(blk, buf):
        start = pl.multiple_of(blk * tk, tk)
        vt1 = jnp.concatenate([vt_ref[0, :, pl.ds(start, tk)], ones_rows], axis=0)
        for hh in range(2):
            st = st_ref[buf, hh]
            m_old = m_ref[hh]
            m_new = jnp.maximum(m_old, jnp.max(st, axis=0, keepdims=True))
            alpha = jnp.exp2(m_old - m_new)
            p = jnp.exp2(st - m_new).astype(BF16)
            acc_ref[hh] = alpha * acc_ref[hh] + jnp.dot(vt1, p, preferred_element_type=F32)
            m_ref[hh] = m_new

    scores(0, 0)

    def body(i2, carry):
        i = 2 * i2
        scores(i + 1, 1)
        consume(i, 0)
        scores(jnp.minimum(i + 2, nk - 1), 0)
        consume(i + 1, 1)
        return carry

    lax.fori_loop(0, nk // 2, body, 0, unroll=4)
    o = jnp.concatenate([acc_ref[hh, :HEAD_DIM] / acc_ref[hh, HEAD_DIM:HEAD_DIM + 1] for hh in range(2)],
                        axis=0)
    o_ref[0] = jnp.transpose(o).astype(o_ref.dtype)


def _attention(q, kpad, vt, tq, tk):
    b, l, _ = q.shape
    nq = l // tq
    nk = l // tk
    assert nk % 4 == 0, "the kv loop handles two score buffers per trip, two trips per unrolled body"
    kern = functools.partial(_attn_kernel, tk=tk, nk=nk)
    return pl.pallas_call(
        kern,
        grid=(b, ATTN_WIDTH // LANES, nq),
        in_specs=[
            pl.BlockSpec((1, tq, LANES), lambda i, j, n: (i, n, j)),
            pl.BlockSpec((1, l, LANES), lambda i, j, n: (i, 0, 2 * (j // 2))),
            pl.BlockSpec((1, l, LANES), lambda i, j, n: (i, 0, 2 * (j // 2) + 1)),
            pl.BlockSpec((1, HEAD_DIM, l), lambda i, j, n: (i, j // 2, 0)),
        ],
        out_specs=pl.BlockSpec((1, tq, LANES), lambda i, j, n: (i, n, j)),
        out_shape=jax.ShapeDtypeStruct((b, l, ATTN_WIDTH), BF16),
        scratch_shapes=[pltpu.VMEM((2, 1, tq), F32),
                        pltpu.VMEM((2, HEAD_DIM + ONES_ROWS, tq), F32),
                        pltpu.VMEM((2, 2, tk, tq), F32)],
        compiler_params=_cparams(("parallel", "parallel", "parallel")),
        name="attention",
    )(q, kpad, kpad, vt)


def _mix_out_kernel(x_ref, ys_ref, o_ref, wglu_ref, bglu_ref, gs_ref, go_ref, wos_ref, woo_ref, g2_ref,
                    x1_ref, h2_ref):
    s = jax.nn.gelu(ys_ref[0].astype(F32))
    gate = jnp.dot(s.astype(BF16), wglu_ref[...], preferred_element_type=F32) + bglu_ref[...]
    s = s * jax.nn.sigmoid(gate)
    sn = _rms(s, gs_ref[...]).astype(BF16)
    on = _rms(o_ref[0].astype(F32), go_ref[...]).astype(BF16)
    x1 = (x_ref[0]
          + jnp.dot(sn, wos_ref[...], preferred_element_type=F32)
          + jnp.dot(on, woo_ref[...], preferred_element_type=F32))
    x1_ref[0] = x1
    h2_ref[0] = _rms(x1, g2_ref[...]).astype(BF16)


def _mix_out(x, ys, o, w_glu, b_glu, gs, go, wo_s, wo_o, g2, tm):
    b, l, _ = x.shape
    tok = lambda w: pl.BlockSpec((1, tm, w), lambda i, j: (i, j, 0))
    return pl.pallas_call(
        _mix_out_kernel,
        grid=(b, l // tm),
        in_specs=[
            tok(D_MODEL), tok(SSM_WIDTH), tok(ATTN_WIDTH),
            _const_spec((SSM_WIDTH, SSM_WIDTH)), _const_spec((1, SSM_WIDTH)),
            _const_spec((1, SSM_WIDTH)), _const_spec((1, ATTN_WIDTH)),
            _const_spec((SSM_WIDTH, D_MODEL)), _const_spec((ATTN_WIDTH, D_MODEL)),
            _const_spec((1, D_MODEL)),
        ],
        out_specs=[tok(D_MODEL), tok(D_MODEL)],
        out_shape=[jax.ShapeDtypeStruct((b, l, D_MODEL), F32),
                   jax.ShapeDtypeStruct((b, l, D_MODEL), BF16)],
        compiler_params=_cparams(("parallel", "parallel")),
        name="mix_out",
    )(x, ys, o, w_glu, b_glu, gs, go, wo_s, wo_o, g2)


def _ffn_kernel(h2_ref, hp_ref, hn_ref, x1_ref, wg_ref, wu_ref, cw_ref, cb_ref, wd_ref, gf_ref,
                y_ref, acc_ref):
    i = pl.program_id(1)
    last = pl.num_programs(1) - 1
    h2 = h2_ref[0]
    tm = h2.shape[0]
    hcat = jnp.concatenate([hp_ref[0], h2, hn_ref[0]], axis=0)
    rows = tm + 2 * HALO
    row = lax.broadcasted_iota(jnp.int32, (tm, 1), 0)
    keep_prev = jnp.logical_not(jnp.logical_and(row == 0, i == 0))
    keep_next = jnp.logical_not(jnp.logical_and(row == tm - 1, i == last))
    acc_ref[...] = x1_ref[0]
    for c in range(D_FF // FF_CHUNK):
        cs = slice(c * FF_CHUNK, (c + 1) * FF_CHUNK)
        g = jnp.dot(hcat, wg_ref[:, cs], preferred_element_type=F32)
        g_prev = jnp.where(keep_prev, pltpu.roll(g, 1, 0)[HALO:HALO + tm], 0.0)
        g_next = jnp.where(keep_next, pltpu.roll(g, rows - 1, 0)[HALO:HALO + tm], 0.0)
        conv = (cb_ref[:, cs] + g_prev * cw_ref[0:1, cs] + g[HALO:HALO + tm] * cw_ref[1:2, cs]
                + g_next * cw_ref[2:3, cs])
        up = jnp.dot(h2, wu_ref[:, cs], preferred_element_type=F32)
        act = (jax.nn.silu(conv) * up).astype(BF16)
        acc_ref[...] += jnp.dot(act, wd_ref[cs, :], preferred_element_type=F32)
    y_ref[0] = _rms(acc_ref[...], gf_ref[...])


def _ffn(h2, x1, w_gate, w_up, conv_w, conv_b, w_down, gf, tm):
    b, l, _ = x1.shape
    nh = tm // HALO
    nblk = l // HALO
    tok = lambda: pl.BlockSpec((1, tm, D_MODEL), lambda i, j: (i, j, 0))
    return pl.pallas_call(
        _ffn_kernel,
        grid=(b, l // tm),
        in_specs=[
            tok(),
            pl.BlockSpec((1, HALO, D_MODEL), lambda i, j: (i, jnp.maximum(j * nh - 1, 0), 0)),
            pl.BlockSpec((1, HALO, D_MODEL), lambda i, j: (i, jnp.minimum((j + 1) * nh, nblk - 1), 0)),
            tok(),
            _const_spec((D_MODEL, D_FF)), _const_spec((D_MODEL, D_FF)),
            _const_spec((8, D_FF)), _const_spec((1, D_FF)),
            _const_spec((D_FF, D_MODEL)), _const_spec((1, D_MODEL)),
        ],
        out_specs=tok(),
        out_shape=jax.ShapeDtypeStruct((b, l, D_MODEL), F32),
        scratch_shapes=[pltpu.VMEM((tm, D_MODEL), F32)],
        compiler_params=_cparams(("parallel", "parallel")),
        name="ffn",
    )(h2, h2, h2, x1, w_gate, w_up, conv_w, conv_b, w_down, gf)


def _rope_tables(seq_len):
    rows = seq_len // GRID_W
    row_id = jnp.broadcast_to(jnp.arange(rows)[:, None], (rows, GRID_W)).reshape(-1).astype(F32)
    col_id = jnp.broadcast_to(jnp.arange(GRID_W)[None, :], (rows, GRID_W)).reshape(-1).astype(F32)
    axis_dim = HEAD_DIM // 2
    inv_freq = ROPE_THETA ** (-jnp.arange(0, axis_dim, 2, dtype=F32) / axis_dim)
    ang = jnp.concatenate([row_id[:, None] * inv_freq, col_id[:, None] * inv_freq], axis=-1)
    cos, sin = jnp.cos(ang), jnp.sin(ang)
    cos_t = jnp.tile(cos, (1, LANES // (HEAD_DIM // 2)))
    sin_t = jnp.tile(jnp.concatenate([-sin, sin], axis=-1), (1, LANES // HEAD_DIM))
    return cos_t, sin_t


def _encoder(x, p, tm, tq, tk, tm_ffn):
    b, l, _ = x.shape
    cos_t, sin_t = _rope_tables(l)
    u, q, kpad, vt = _in_proj(x, p["g1"], p["w_in"], p["ones"], p["qkg"], cos_t, sin_t, tm)

    chain = 1 if b % 8 == 0 else 8 // b
    nb = b * chain
    nc = l // (chain * CHUNK)
    a_mat = _ssm_decay_rows(p["lam_dt"], nc * CHUNK)
    ug = u.reshape(nb, nc, CHUNK, SSM_GROUPS, SSM_GROUP)
    ug = jnp.transpose(ug, (3, 1, 0, 2, 4)).reshape(SSM_GROUPS, nc * nb, CHUNK_W)
    yg = _ssm(ug, p["t_mat"], p["w_mat"], p["v_mat"], a_mat, nb, nc, chain)
    ys = jnp.transpose(yg.reshape(SSM_GROUPS, nc, nb, CHUNK, SSM_GROUP), (2, 1, 3, 0, 4)).reshape(b, l, SSM_WIDTH)

    o = _attention(q, kpad, vt, tq, tk)
    x1, h2 = _mix_out(x, ys, o, p["w_glu"], p["b_glu"], p["gs"], p["go"], p["wo_s"], p["wo_o"], p["g2"], tm)
    return _ffn(h2, x1, p["w_gate"], p["w_up"], p["conv_w"], p["conv_b"], p["w_down"], p["gf"], tm_ffn)


def _prepare(norm1_g, w_in, lam_re, lam_im, log_dt, b_re, b_im, c_re, c_im, d_skip, w_glu, b_glu,
             q_norm_g, k_norm_g, ssm_out_g, attn_out_g, w_out, norm2_g, w_gate, w_up, conv_w, conv_b,
             w_down, final_norm_g):
    i = 0
    t_mat, w_mat, v_mat, lam_dt = _ssm_matrices(lam_re[i], lam_im[i], log_dt[i], b_re[i], b_im[i],
                                                c_re[i], c_im[i], d_skip[i])
    head = jnp.arange(QK_WIDTH) // HEAD_DIM
    ones = (head[:, None] == head[None, :]).astype(BF16)
    qkg = jnp.concatenate([jnp.tile(q_norm_g[i], N_HEADS), jnp.tile(k_norm_g[i], N_KV_HEADS)])[None].astype(F32)
    row = lambda v: v[None].astype(F32)
    return dict(
        g1=row(norm1_g[i]), w_in=w_in[i].astype(BF16), ones=ones, qkg=qkg,
        t_mat=t_mat, w_mat=w_mat, v_mat=v_mat, lam_dt=lam_dt,
        w_glu=w_glu[i].astype(BF16), b_glu=row(b_glu[i]), gs=row(ssm_out_g[i]), go=row(attn_out_g[i]),
        wo_s=w_out[i, :SSM_WIDTH].astype(BF16), wo_o=w_out[i, SSM_WIDTH:].astype(BF16), g2=row(norm2_g[i]),
        w_gate=w_gate[i].astype(BF16), w_up=w_up[i].astype(BF16),
        conv_w=jnp.concatenate([conv_w[i].astype(F32), jnp.zeros((5, D_FF), F32)], axis=0),
        conv_b=row(conv_b[i]), w_down=w_down[i].astype(BF16), gf=row(final_norm_g),
    )


def kernel(x_prompt, x_sample, norm1_g, w_in, lam_re, lam_im, log_dt, b_re, b_im, c_re, c_im, d_skip, w_glu, b_glu, q_norm_g, k_norm_g, ssm_out_g, attn_out_g, w_out, norm2_g, w_gate, w_up, conv_w, conv_b, w_down, final_norm_g):
    p = _prepare(norm1_g, w_in, lam_re, lam_im, log_dt, b_re, b_im, c_re, c_im, d_skip, w_glu, b_glu,
                 q_norm_g, k_norm_g, ssm_out_g, attn_out_g, w_out, norm2_g, w_gate, w_up, conv_w, conv_b,
                 w_down, final_norm_g)
    y_prompt = _encoder(x_prompt, p, tm=512, tq=512, tk=512, tm_ffn=512)
    y_sample = _encoder(x_sample, p, tm=512, tq=512, tk=512, tm_ffn=512)
    return (y_prompt, y_sample)
```
